```python
import math
import jax, jax.numpy as jnp
from jax import lax
import numpy as np

D_MODEL = 1024
BATCH = 4
SEQ = 8192
DEPTH = 1
DEC_BATCH = 128
DEC_SEQ = 4
PAST_LEN = 8192
PAGE_SIZE = 128

DN_HEADS = 4
DN_DK = 128
DN_DV = 128
QK_W = DN_HEADS * DN_DK
DN_WIDTH = DN_HEADS * DN_DV
CONV_W = 4
CONV_CH = 2 * QK_W + DN_WIDTH
DN_CHUNK = 64
SW_HEADS = 8
SW_HD = 64
SW_WIDTH = SW_HEADS * SW_HD
DILATIONS = ((128, 1), (512, 4), (2048, 16))
WIN_MAX = 2048
D_MIX = DN_WIDTH + SW_WIDTH
D_IN = CONV_CH + DN_WIDTH + 2 * DN_HEADS + 3 * SW_WIDTH + SW_WIDTH
NORM_EPS = 1e-6

kernel_name = 'hymba_gdn_dilated_swa_step'


def rmsnorm(x, g):
    xf = x.astype(jnp.float32)
    xf = xf * lax.rsqrt(jnp.mean(xf * xf, axis=-1, keepdims=True) + NORM_EPS)
    return xf.astype(x.dtype) * g


def l2normalize(x):
    xf = x.astype(jnp.float32)
    return xf * lax.rsqrt(jnp.sum(xf * xf, axis=-1, keepdims=True) + NORM_EPS)


def causal_short_conv(x_ext, w):
    L = x_ext.shape[1] - (CONV_W - 1)
    y = w[0] * x_ext[:, 0:L]
    for j in range(1, CONV_W):
        y = y + w[j] * x_ext[:, j:j + L]
    return jax.nn.silu(y)


def gated_delta_rule(q, k, v, g, beta, s0):
    B_, L, H, _ = q.shape
    dv = v.shape[-1]
    C = min(DN_CHUNK, L)
    n = -(-L // C)
    pad = n * C - L
    f32 = jnp.float32

    def blocks(t):
        t = jnp.pad(t.astype(f32), [(0, 0), (0, pad)] + [(0, 0)] * (t.ndim - 2))
        t = jnp.moveaxis(t, 1, 2)
        return t.reshape(t.shape[:2] + (n, C) + t.shape[3:])

    q, k, v, g, beta = (blocks(t) for t in (q, k, v, g, beta))
    gc = jnp.cumsum(g, axis=-1)
    incl = jnp.tril(jnp.ones((C, C), dtype=bool))
    strict = jnp.tril(jnp.ones((C, C), dtype=bool), -1)
    decay = jnp.exp(jnp.where(incl, gc[..., :, None] - gc[..., None, :], -jnp.inf))
    kb = k * beta[..., None]
    a_mat = jnp.where(strict, jnp.einsum('bhnid,bhnjd->bhnij', kb, k) * decay, 0.0)
    rhs = jnp.concatenate([v * beta[..., None], kb * jnp.exp(gc)[..., None]], axis=-1)
    sol = lax.linalg.triangular_solve(jnp.eye(C, dtype=f32) + a_mat, rhs,
                                      left_side=True, lower=True, unit_diagonal=True)
    u, w = sol[..., :dv], sol[..., dv:]
    attn = jnp.einsum('bhnid,bhnjd->bhnij', q, k) * decay
    q_dec = q * jnp.exp(gc)[..., None]
    k_dec = k * jnp.exp(gc[..., -1:] - gc)[..., None]
    g_last = jnp.exp(gc[..., -1])

    def step(S, xs):
        u_c, w_c, q_c, k_c, a_c, gl = xs
        v_new = u_c - jnp.einsum('bhck,bhkv->bhcv', w_c, S)
        o_c = jnp.einsum('bhck,bhkv->bhcv', q_c, S) + jnp.einsum('bhij,bhjv->bhiv', a_c, v_new)
        S = S * gl[..., None, None] + jnp.einsum('bhck,bhcv->bhkv', k_c, v_new)
        return S, o_c

    xs = tuple(jnp.moveaxis(t, 2, 0) for t in (u, w, q_dec, k_dec, attn, g_last))
    S, o = lax.scan(step, s0.astype(f32), xs)
    o = jnp.moveaxis(o, 0, 2).reshape(B_, H, n * C, dv)[:, :, :L]
    return jnp.moveaxis(o, 2, 1), S


def banded_window_attn(q, k, v, steps):
    N, n, H, hd = q.shape
    blk = steps
    nb = -(-n // blk)
    pad = nb * blk - n
    f32 = jnp.float32

    def to_blocks(t):
        t = jnp.pad(t, ((0, 0), (0, pad), (0, 0), (0, 0)))
        return t.reshape(N, nb, blk, H, hd)

    qb, kb, vb = to_blocks(q), to_blocks(k), to_blocks(v)
    prev = lambda t: jnp.pad(t, ((0, 0), (1, 0), (0, 0), (0, 0), (0, 0)))[:, :-1]
    kk = jnp.concatenate([prev(kb), kb], axis=2)
    vv = jnp.concatenate([prev(vb), vb], axis=2)
    s = jnp.einsum('ncqhd,nckhd->nchqk', qb, kk).astype(f32) * (hd ** -0.5)
    qi = jnp.arange(blk)[:, None]
    kj = jnp.arange(2 * blk)[None, :]
    band = (kj >= blk + qi - steps) & (kj <= blk + qi)
    has_prev = (jnp.arange(nb) > 0)[:, None, None] | (kj >= blk)[None]
    mask = band[None] & has_prev
    s = jnp.where(mask[None, :, None], s, -jnp.inf)
    m = jnp.max(s, axis=-1)
    p = jnp.exp(s - m[..., None])
    l = jnp.sum(p, axis=-1)
    num = jnp.einsum('nchqk,nckhd->ncqhd', p, vv.astype(f32)).reshape(N, nb * blk, H, hd)[:, :n]
    m = jnp.moveaxis(m, 2, 3).reshape(N, nb * blk, H)[:, :n]
    l = jnp.moveaxis(l, 2, 3).reshape(N, nb * blk, H)[:, :n]
    return num, m, l


def dilated_branch_prompt(q, k, v, steps, d):
    B_, S_, H, hd = q.shape
    n = S_ // d
    to_sub = lambda t: t.reshape(B_, n, d, H, hd).transpose(0, 2, 1, 3, 4).reshape(B_ * d, n, H, hd)
    num, m, l = banded_window_attn(to_sub(q), to_sub(k), to_sub(v), steps)
    num = num.reshape(B_, d, n, H, hd).transpose(0, 2, 1, 3, 4).reshape(B_, S_, H, hd)
    m = m.reshape(B_, d, n, H).transpose(0, 2, 1, 3).reshape(B_, S_, H)
    l = l.reshape(B_, d, n, H).transpose(0, 2, 1, 3).reshape(B_, S_, H)
    return num, m, l


def dilated_branch_sample(q, k_ext, v_ext, steps, d):
    T = q.shape[1]
    hd = q.shape[-1]
    lbuf = k_ext.shape[1] - T
    idx = lbuf + jnp.arange(T)[:, None] - d * jnp.arange(steps + 1)[None, :]
    valid = idx >= 0
    idx = jnp.maximum(idx, 0)
    kg = jnp.take(k_ext, idx, axis=1)
    vg = jnp.take(v_ext, idx, axis=1)
    s = jnp.einsum('bthd,btjhd->bthj', q, kg).astype(jnp.float32) * (hd ** -0.5)
    s = jnp.where(valid[None, :, None, :], s, -jnp.inf)
    m = jnp.max(s, axis=-1)
    p = jnp.exp(s - m[..., None])
    l = jnp.sum(p, axis=-1)
    num = jnp.einsum('bthj,btjhd->bthd', p, vg.astype(jnp.float32))
    return num, m, l


def merge_dilations(parts):
    ms = jnp.stack([m for _, m, _ in parts])
    mx = jnp.max(ms, axis=0)
    wts = jnp.exp(ms - mx)
    num = sum(wts[i][..., None] * parts[i][0] for i in range(len(parts)))
    den = sum(wts[i] * parts[i][2] for i in range(len(parts)))
    return num / den[..., None]


def hybrid_layer(x, conv_prefix, s0, k_win, v_win, norm_g, w_in, conv_w, a_log, dt_bias, dn_norm_g, w_out):
    B_, L, _ = x.shape
    f32 = jnp.float32
    h = rmsnorm(x, norm_g)
    proj = jnp.einsum('bld,de->ble', h, w_in)
    o0 = CONV_CH
    o1 = o0 + DN_WIDTH
    o2 = o1 + DN_HEADS
    o3 = o2 + DN_HEADS
    o4 = o3 + 3 * SW_WIDTH
    qkv_raw, z_a, a_in, b_in = proj[..., :o0], proj[..., o0:o1], proj[..., o1:o2], proj[..., o2:o3]
    qkv_b, z_b = proj[..., o3:o4], proj[..., o4:]

    x_ext = jnp.concatenate([conv_prefix.astype(x.dtype), qkv_raw], axis=1)
    qkv = causal_short_conv(x_ext, conv_w)
    new_conv = x_ext[:, -(CONV_W - 1):]
    qa, ka, va = jnp.split(qkv, [QK_W, 2 * QK_W], axis=-1)
    qa = l2normalize(qa.reshape(B_, L, DN_HEADS, DN_DK)) * (DN_DK ** -0.5)
    ka = l2normalize(ka.reshape(B_, L, DN_HEADS, DN_DK))
    va = va.reshape(B_, L, DN_HEADS, DN_DV)
    g = -jnp.exp(a_log.astype(f32)) * jax.nn.softplus(a_in.astype(f32) + dt_bias.astype(f32))
    beta = jax.nn.sigmoid(b_in.astype(f32))
    o_a, s_new = gated_delta_rule(qa, ka, va, g, beta, s0)
    o_a = rmsnorm(o_a.astype(x.dtype), dn_norm_g) * jax.nn.silu(z_a.reshape(B_, L, DN_HEADS, DN_DV))

    qb, kb, vb = (t.reshape(B_, L, SW_HEADS, SW_HD) for t in jnp.split(qkv_b, 3, axis=-1))
    if k_win is None:
        parts = [dilated_branch_prompt(qb, kb, vb, w // d, d) for w, d in DILATIONS]
        keep = min(WIN_MAX, L)
        new_k, new_v = kb[:, L - keep:], vb[:, L - keep:]
    else:
        k_ext = jnp.concatenate([k_win.astype(x.dtype), kb], axis=1)
        v_ext = jnp.concatenate([v_win.astype(x.dtype), vb], axis=1)
        parts = [dilated_branch_sample(qb, k_ext, v_ext, w // d, d) for w, d in DILATIONS]
        keep = k_win.shape[1]
        new_k, new_v = k_ext[:, -keep:], v_ext[:, -keep:]
    o_b = merge_dilations(parts).astype(x.dtype) * jax.nn.silu(z_b.reshape(B_, L, SW_HEADS, SW_HD))

    mix = jnp.concatenate([o_a.reshape(B_, L, DN_WIDTH), o_b.reshape(B_, L, SW_WIDTH)], axis=-1)
    y = x + jnp.einsum('ble,ed->bld', mix, w_out)
    return y, new_conv, s_new.astype(x.dtype), new_k, new_v


def setup_inputs(seed: int = 0) -> dict:
    key = jax.random.key(seed)
    ks = jax.random.split(key, 16)
    lbuf = min(WIN_MAX, PAST_LEN)
    nrm = jax.random.normal
    x_prompt = nrm(ks[0], (BATCH, SEQ, D_MODEL), jnp.float32)
    x_sample = nrm(ks[1], (DEC_BATCH, DEC_SEQ, D_MODEL), jnp.float32)
    state_conv = nrm(ks[2], (DEPTH, DEC_BATCH, CONV_W - 1, CONV_CH), jnp.float32)
    state_delta = 0.1 * nrm(ks[3], (DEPTH, DEC_BATCH, DN_HEADS, DN_DK, DN_DV), jnp.float32)
    cache_k_win = nrm(ks[4], (DEPTH, DEC_BATCH, lbuf, SW_HEADS, SW_HD), jnp.float32)
    cache_v_win = nrm(ks[5], (DEPTH, DEC_BATCH, lbuf, SW_HEADS, SW_HD), jnp.float32)
    norm_g = 1.0 + 0.01 * nrm(ks[6], (DEPTH, D_MODEL), jnp.float32)
    final_norm_g = 1.0 + 0.01 * nrm(ks[7], (D_MODEL,), jnp.float32)
    w_in = nrm(ks[8], (DEPTH, D_MODEL, D_IN), jnp.float32) * (D_MODEL ** -0.5)
    conv_w = nrm(ks[9], (DEPTH, CONV_W, CONV_CH), jnp.float32) * (CONV_W ** -0.5)
    a_log = jnp.log(jax.random.uniform(ks[10], (DEPTH, DN_HEADS), jnp.float32, 1.0, 16.0))
    dt = jnp.exp(jax.random.uniform(ks[11], (DEPTH, DN_HEADS), jnp.float32, math.log(1e-3), math.log(1e-1)))
    dt_bias = dt + jnp.log(-jnp.expm1(-dt))
    dn_norm_g = 1.0 + 0.01 * nrm(ks[12], (DEPTH, DN_DV), jnp.float32)
    w_out = nrm(ks[13], (DEPTH, D_MIX, D_MODEL), jnp.float32) * (D_MIX ** -0.5)
    return {'x_prompt': x_prompt, 'x_sample': x_sample, 'state_conv': state_conv,
            'state_delta': state_delta, 'cache_k_win': cache_k_win, 'cache_v_win': cache_v_win,
            'norm_g': norm_g, 'final_norm_g': final_norm_g, 'w_in': w_in, 'conv_w': conv_w,
            'a_log': a_log, 'dt_bias': dt_bias, 'dn_norm_g': dn_norm_g, 'w_out': w_out}


def reference(x_prompt, x_sample, state_conv, state_delta, cache_k_win, cache_v_win,
              norm_g, final_norm_g, w_in, conv_w, a_log, dt_bias, dn_norm_g, w_out):
    yp, ys = x_prompt, x_sample
    bp = x_prompt.shape[0]
    p_conv, p_delta, p_k, p_v = [], [], [], []
    s_conv, s_delta, s_k, s_v = [], [], [], []
    for l in range(DEPTH):
        weights = (norm_g[l], w_in[l], conv_w[l], a_log[l], dt_bias[l], dn_norm_g[l], w_out[l])
        zero_conv = jnp.zeros((bp, CONV_W - 1, CONV_CH), x_prompt.dtype)
        zero_state = jnp.zeros((bp, DN_HEADS, DN_DK, DN_DV), jnp.float32)
        yp, c, s, k, v = hybrid_layer(yp, zero_conv, zero_state, None, None, *weights)
        p_conv.append(c); p_delta.append(s); p_k.append(k); p_v.append(v)
        ys, c, s, k, v = hybrid_layer(ys, state_conv[l], state_delta[l], cache_k_win[l], cache_v_win[l], *weights)
        s_conv.append(c); s_delta.append(s); s_k.append(k); s_v.append(v)
    y_prompt = rmsnorm(yp, final_norm_g)
    y_sample = rmsnorm(ys, final_norm_g)
    new_conv_prompt = jnp.stack(p_conv)
    new_delta_prompt = jnp.stack(p_delta)
    new_k_win_prompt = jnp.stack(p_k)
    new_v_win_prompt = jnp.stack(p_v)
    new_conv_sample = jnp.stack(s_conv)
    new_delta_sample = jnp.stack(s_delta)
    new_k_win_sample = jnp.stack(s_k)
    new_v_win_sample = jnp.stack(s_v)
    return (y_prompt, y_sample, new_conv_prompt, new_delta_prompt, new_k_win_prompt, new_v_win_prompt,
            new_conv_sample, new_delta_sample, new_k_win_sample, new_v_win_sample)
```

```python
import functools

import jax
import jax.numpy as jnp
from jax import lax
from jax.experimental import pallas as pl
from jax.experimental.pallas import tpu as pltpu

F32 = jnp.float32
BF16 = jnp.bfloat16

D_MODEL = 1024
DN_HEADS = 4
DN_DK = 128
DN_DV = 128
QK_W = DN_HEADS * DN_DK
DN_WIDTH = DN_HEADS * DN_DV
CONV_W = 4
CONV_CH = 2 * QK_W + DN_WIDTH
DN_CHUNK = 64
SW_HEADS = 8
SW_HD = 64
SW_WIDTH = SW_HEADS * SW_HD
DILATIONS = ((128, 1), (512, 4), (2048, 16))
WIN_MAX = 2048
NORM_EPS = 1e-6

LANES = 128
SUBLANES = 8
VMEM_LIMIT = 56 * 1024 * 1024

C_RAW = 0
C_ZA = CONV_CH
C_QKVB = C_ZA + DN_WIDTH
C_ZB = C_QKVB + 3 * SW_WIDTH
C_AB = C_ZB + SW_WIDTH
W_COLS = C_AB + LANES
HALO = SUBLANES


def _dot(a, b):
    return lax.dot_general(a, b, (((a.ndim - 1,), (0,)), ((), ())), preferred_element_type=F32)


def _dot_nt(a, b):
    return lax.dot_general(a, b, (((1,), (1,)), ((), ())), preferred_element_type=F32)


def _dot_tn(a, b):
    return lax.dot_general(a, b, (((0,), (0,)), ((), ())), preferred_element_type=F32)


def _split2(x):
    hi = x.astype(BF16)
    lo = (x - hi.astype(F32)).astype(BF16)
    return hi, lo


def _dot3(a, b):
    ah, al = _split2(a)
    bh, bl = _split2(b)
    return _dot(al, bh) + _dot(ah, bl) + _dot(ah, bh)


def _dot_exact_lhs(a16, x):
    hi = x.astype(BF16)
    r1 = x - hi.astype(F32)
    mid = r1.astype(BF16)
    lo = (r1 - mid.astype(F32)).astype(BF16)
    return _dot(a16, lo) + _dot(a16, mid) + _dot(a16, hi)


def _sigmoid(x):
    return 1.0 / (1.0 + jnp.exp(-x))


def _silu(x):
    return x * _sigmoid(x)


def _proj_kernel(x_ref, g_ref, w_ref, raw_ref, za_ref, zb_ref, ab_ref, *rest, dils, tm):
    x = x_ref[0]
    ms = jnp.mean(x * x, axis=-1, keepdims=True)
    h = ((x * lax.rsqrt(ms + NORM_EPS)) * g_ref[...]).astype(BF16)
    raw_ref[0] = _dot(h, w_ref[:, C_RAW:C_ZA])
    za_ref[0] = _dot(h, w_ref[:, C_ZA:C_QKVB])
    zb_ref[0] = _dot(h, w_ref[:, C_ZB:C_AB])
    ab_ref[0] = _dot(h, w_ref[:, C_AB:W_COLS])
    qkvb = _dot(h, w_ref[:, C_QKVB:C_ZB])
    if not dils:
        (qb_ref,) = rest
        qb_ref[0] = qkvb
        return
    kv_ref = rest[len(dils)]
    scr = rest[len(dils) + 1]
    kv_ref[0] = qkvb[:, SW_WIDTH:]
    nslab = 3 * SW_WIDTH // LANES
    for j in range(nslab):
        scr[j] = qkvb[:, j * LANES:(j + 1) * LANES]
    for qb_ref, d in zip(rest[:len(dils)], dils):
        if d == 1:
            qb_ref[0, 0] = qkvb.astype(BF16)
            continue
        for r in range(d):
            for j in range(nslab):
                qb_ref[0, r, :, j * LANES:(j + 1) * LANES] = (
                    scr[j, pl.ds(r, tm // d, stride=d), :].astype(BF16))


def _proj(x, g, w, *, dils, keep, tm):
    B, L, _ = x.shape
    nt = L // tm
    kern = functools.partial(_proj_kernel, dils=dils, tm=tm)
    row = lambda b, t: (b, t, 0)
    out_shape = [jax.ShapeDtypeStruct((B, L, CONV_CH), F32),
                 jax.ShapeDtypeStruct((B, L, DN_WIDTH), F32),
                 jax.ShapeDtypeStruct((B, L, SW_WIDTH), F32),
                 jax.ShapeDtypeStruct((B, L, LANES), F32)]
    out_specs = [pl.BlockSpec((1, tm, CONV_CH), row),
                 pl.BlockSpec((1, tm, DN_WIDTH), row),
                 pl.BlockSpec((1, tm, SW_WIDTH), row),
                 pl.BlockSpec((1, tm, LANES), row)]
    scratch = []
    if dils:
        for d in dils:
            out_shape.append(jax.ShapeDtypeStruct((B, d, L // d, 3 * SW_WIDTH), BF16))
            out_specs.append(pl.BlockSpec((1, d, tm // d, 3 * SW_WIDTH), lambda b, t: (b, 0, t, 0)))
        first = (L - keep) // tm
        out_shape.append(jax.ShapeDtypeStruct((B, keep, 2 * SW_WIDTH), F32))
        out_specs.append(pl.BlockSpec((1, tm, 2 * SW_WIDTH),
                                      lambda b, t: (b, jnp.maximum(t - first, 0), 0)))
        scratch.append(pltpu.VMEM((3 * SW_WIDTH // LANES, tm, LANES), F32))
    else:
        out_shape.append(jax.ShapeDtypeStruct((B, L, 3 * SW_WIDTH), F32))
        out_specs.append(pl.BlockSpec((1, tm, 3 * SW_WIDTH), row))
    return pl.pallas_call(
        kern,
        grid=(B, nt),
        in_specs=[pl.BlockSpec((1, tm, D_MODEL), row),
                  pl.BlockSpec((1, D_MODEL), lambda b, t: (0, 0)),
                  pl.BlockSpec((D_MODEL, W_COLS), lambda b, t: (0, 0))],
        out_specs=out_specs,
        out_shape=out_shape,
        scratch_shapes=scratch,
        compiler_params=pltpu.CompilerParams(
            dimension_semantics=("arbitrary", "arbitrary"), vmem_limit_bytes=VMEM_LIMIT),
        name="proj_prompt" if dils else "proj_sample",
    )(x, g, w)


def _tri_inverse(a, ri, ci):
    c = a.shape[0]
    x = ri ^ ci
    eye = jnp.where(ri == ci, 1.0, 0.0).astype(F32)
    t = eye - jnp.where((x >> 1) == 0, a, 0.0)
    k = 1
    while (2 << k) <= c:
        a_off = jnp.where((x >> k) == 1, a, 0.0)
        t = t - _dot3(_dot3(t, a_off), t)
        k += 1
    return t


def _gdn_kernel(raw_ref, za_ref, ab_ref, cw_ref, alog_ref, dtb_ref, ng_ref, halo0_ref, s0_ref,
                o_ref, s_out_ref, ext_scr, qkv_scr, gb_scr, s_scr, *, C, TL, valid_len):
    t_idx = pl.program_id(1)

    @pl.when(t_idx == 0)
    def _():
        ext_scr[0:HALO, :] = halo0_ref[0]
        s_scr[...] = s0_ref[0]

    ext_scr[HALO:HALO + TL, :] = raw_ref[0]
    cw = cw_ref[...]
    y = cw[0:1] * ext_scr[HALO - 3:HALO - 3 + TL, :]
    y = y + cw[1:2] * ext_scr[HALO - 2:HALO - 2 + TL, :]
    y = y + cw[2:3] * ext_scr[HALO - 1:HALO - 1 + TL, :]
    y = y + cw[3:4] * ext_scr[HALO:HALO + TL, :]
    ext_scr[0:HALO, :] = ext_scr[TL:TL + HALO, :]
    qkv = _silu(y)
    for h in range(DN_HEADS):
        sl = slice(h * DN_DK, (h + 1) * DN_DK)
        qh = qkv[:, sl]
        qkv_scr[:, sl] = qh * lax.rsqrt(jnp.sum(qh * qh, -1, keepdims=True) + NORM_EPS) * (DN_DK ** -0.5)
        sl = slice(QK_W + h * DN_DK, QK_W + (h + 1) * DN_DK)
        kh = qkv[:, sl]
        qkv_scr[:, sl] = kh * lax.rsqrt(jnp.sum(kh * kh, -1, keepdims=True) + NORM_EPS)
    qkv_scr[:, 2 * QK_W:] = qkv[:, 2 * QK_W:]

    ab = ab_ref[0]
    xg = ab + dtb_ref[...]
    softplus = jnp.maximum(xg, 0.0) + jnp.log1p(jnp.exp(-jnp.abs(xg)))
    g = -jnp.exp(alog_ref[...]) * softplus
    lane = lax.broadcasted_iota(jnp.int32, (TL, LANES), 1)
    gb = jnp.where(lane < DN_HEADS, g, _sigmoid(ab))
    if valid_len < TL:
        rowi = lax.broadcasted_iota(jnp.int32, (TL, LANES), 0)
        gb = jnp.where(rowi < valid_len, gb, 0.0)
    gb_scr[...] = gb

    ri = lax.broadcasted_iota(jnp.int32, (C, C), 0)
    ci = lax.broadcasted_iota(jnp.int32, (C, C), 1)
    incl = ci <= ri
    strict = ci < ri
    diag = ci == ri
    tri16 = jnp.where(incl, 1.0, 0.0).astype(BF16)
    ng = ng_ref[...]

    def chunk(c, carry):
        r0 = pl.multiple_of(c * C, C)
        rows = pl.ds(r0, C)
        gbc = gb_scr[rows, :]
        gc_all = _dot_exact_lhs(tri16, gbc)
        za = za_ref[0, rows, :]
        for h in range(DN_HEADS):
            q = qkv_scr[rows, h * DN_DK:(h + 1) * DN_DK]
            k = qkv_scr[rows, QK_W + h * DN_DK:QK_W + (h + 1) * DN_DK]
            v = qkv_scr[rows, 2 * QK_W + h * DN_DV:2 * QK_W + (h + 1) * DN_DV]
            gc_b = jnp.broadcast_to(gc_all[:, h:h + 1], (C, LANES))
            beta_b = jnp.broadcast_to(gbc[:, DN_HEADS + h:DN_HEADS + h + 1], (C, LANES))
            gc_sq = gc_b[:, :C]
            gc_row = jnp.sum(jnp.where(diag, gc_sq, 0.0), axis=0, keepdims=True)
            dec = jnp.exp(jnp.where(incl, gc_sq - gc_row, -jnp.inf))
            e_gc = jnp.exp(gc_b)
            gl_b = jnp.broadcast_to(gc_b[C - 1:C, :], (C, LANES))
            kb = k * beta_b
            kb16 = kb.astype(BF16)
            k16 = k.astype(BF16)
            a = jnp.where(strict, _dot_nt(kb16, k16) * dec, 0.0)
            attn = _dot_nt(q.astype(BF16), k16) * dec
            t_inv = _tri_inverse(a, ri, ci)
            rhs = jnp.concatenate([v * beta_b, kb * e_gc], axis=1)
            sol = _dot3(t_inv, rhs)
            u = sol[:, :DN_DV]
            w = sol[:, DN_DV:]
            q_dec = q * e_gc
            k_dec = k * jnp.exp(gl_b - gc_b)
            s = s_scr[h]
            s16 = s.astype(BF16)
            v_new = u - _dot(w.astype(BF16), s16)
            vn16 = v_new.astype(BF16)
            o = _dot(q_dec.astype(BF16), s16) + _dot(attn.astype(BF16), vn16)
            gl_s = jnp.broadcast_to(jnp.exp(gl_b[0:1, :]), (DN_DK, DN_DV))
            s_scr[h] = s * gl_s + _dot_tn(k_dec.astype(BF16), vn16)
            on = (o * lax.rsqrt(jnp.mean(o * o, -1, keepdims=True) + NORM_EPS)) * ng
            o_ref[0, rows, h * DN_DV:(h + 1) * DN_DV] = on * _silu(za[:, h * DN_DV:(h + 1) * DN_DV])
        return carry

    lax.fori_loop(0, TL // C, chunk, 0)
    s_out_ref[0] = s_scr[...]


def _gdn(raw, za, ab, conv_w, alog_row, dtb_row, ng_row, halo0, s0, *, C, TL, valid_len, name):
    B, L, _ = raw.shape
    kern = functools.partial(_gdn_kernel, C=C, TL=TL, valid_len=valid_len)
    row = lambda b, t: (b, t, 0)
    const2 = lambda b, t: (0, 0)
    return pl.pallas_call(
        kern,
        grid=(B, L // TL),
        in_specs=[pl.BlockSpec((1, TL, CONV_CH), row),
                  pl.BlockSpec((1, TL, DN_WIDTH), row),
                  pl.BlockSpec((1, TL, LANES), row),
                  pl.BlockSpec((CONV_W, CONV_CH), const2),
                  pl.BlockSpec((1, LANES), const2),
                  pl.BlockSpec((1, LANES), const2),
                  pl.BlockSpec((1, DN_DV), const2),
                  pl.BlockSpec((1, HALO, CONV_CH), lambda b, t: (b, 0, 0)),
                  pl.BlockSpec((1, DN_HEADS, DN_DK, DN_DV), lambda b, t: (b, 0, 0, 0))],
        out_specs=[pl.BlockSpec((1, TL, DN_WIDTH), row),
                   pl.BlockSpec((1, DN_HEADS, DN_DK, DN_DV), lambda b, t: (b, 0, 0, 0))],
        out_shape=[jax.ShapeDtypeStruct((B, L, DN_WIDTH), F32),
                   jax.ShapeDtypeStruct((B, DN_HEADS, DN_DK, DN_DV), F32)],
        scratch_shapes=[pltpu.VMEM((TL + 2 * HALO, CONV_CH), F32),
                        pltpu.VMEM((TL, CONV_CH), F32),
                        pltpu.VMEM((TL, LANES), F32),
                        pltpu.VMEM((DN_HEADS, DN_DK, DN_DV), F32)],
        compiler_params=pltpu.CompilerParams(
            dimension_semantics=("arbitrary", "arbitrary"), vmem_limit_bytes=VMEM_LIMIT),
        name=name,
    )(raw, za, ab, conv_w, alog_row, dtb_row, ng_row, halo0, s0)


def _swa_kernel(q_ref, kp_ref, kc_ref, vp_ref, vc_ref, o_ref, lse_ref, *, d, blk):
    n = pl.program_id(1)
    r = pl.program_id(2)
    qi = lax.broadcasted_iota(jnp.int32, (blk, 2 * blk), 0)
    kj = lax.broadcasted_iota(jnp.int32, (blk, 2 * blk), 1)
    mask = (kj >= qi) & (kj <= qi + blk) & ((kj >= blk) | (n > 0))
    lane = lax.broadcasted_iota(jnp.int32, (blk, LANES), 1)
    lo_half = lane < SW_HD
    half_masks = (jnp.where(lo_half, 1.0, 0.0).astype(BF16), jnp.where(lo_half, 0.0, 1.0).astype(BF16))
    rows = pl.ds(r, blk, stride=d) if d > 1 else pl.ds(0, blk)
    for p in range(SW_WIDTH // LANES):
        sl = slice(p * LANES, (p + 1) * LANES)
        qp = q_ref[0, 0, :, sl]
        kk = jnp.concatenate([kp_ref[0, 0, :, sl], kc_ref[0, 0, :, sl]], axis=0)
        vv = jnp.concatenate([vp_ref[0, 0, :, sl], vc_ref[0, 0, :, sl]], axis=0)
        outs = []
        lses = []
        for half in range(2):
            s = _dot_nt(qp * half_masks[half], kk) * (SW_HD ** -0.5)
            s = jnp.where(mask, s, -jnp.inf)
            m = jnp.max(s, axis=-1, keepdims=True)
            pe = jnp.exp(s - m)
            l = jnp.sum(pe, axis=-1, keepdims=True)
            pv = _dot(pe.astype(BF16), vv)
            outs.append(pv / l)
            lses.append(jnp.broadcast_to(m + jnp.log(l), (blk, LANES)))
        o_ref[0, p, rows, :] = jnp.where(lo_half, outs[0], outs[1])
        lse_ref[0, p, rows, :] = jnp.where(lo_half, lses[0], lses[1])


def _swa(qb, *, d, steps):
    B, _, n, _ = qb.shape
    blk = steps
    nb = n // blk
    L = n * d
    kern = functools.partial(_swa_kernel, d=d, blk=blk)
    cur = lambda c: (lambda b, i, r: (b, r, i, c))
    prev = lambda c: (lambda b, i, r: (b, r, jnp.maximum(i - 1, 0), c))
    blkspec = lambda f: pl.BlockSpec((1, 1, blk, SW_WIDTH), f)
    npair = SW_WIDTH // LANES
    out_spec = pl.BlockSpec((1, npair, blk * d, LANES), lambda b, i, r: (b, 0, i, 0))
    return pl.pallas_call(
        kern,
        grid=(B, nb, d),
        in_specs=[blkspec(cur(0)), blkspec(prev(1)), blkspec(cur(1)), blkspec(prev(2)), blkspec(cur(2))],
        out_specs=[out_spec, out_spec],
        out_shape=[jax.ShapeDtypeStruct((B, npair, L, LANES), F32)] * 2,
        compiler_params=pltpu.CompilerParams(
            dimension_semantics=("arbitrary", "arbitrary", "arbitrary"), vmem_limit_bytes=VMEM_LIMIT),
        name="swa_d%d" % d,
    )(qb, qb, qb, qb, qb)


def _sattn_kernel(qkv_ref, zb_ref, kt_ref, vt_ref, ob_ref, kto_ref, vto_ref, knew_scr, vnew_scr,
                  *, T, W, dil):
    R = T * SW_HEADS
    ncol = W // LANES

    @pl.when(pl.program_id(0) == 0)
    def _():
        knew_scr[...] = jnp.zeros_like(knew_scr)
        vnew_scr[...] = jnp.zeros_like(vnew_scr)

    qkv = qkv_ref[0]
    knew_scr[0:T, :] = qkv[:, SW_WIDTH:2 * SW_WIDTH]
    vnew_scr[0:T, :] = qkv[:, 2 * SW_WIDTH:]
    knew = knew_scr[...]
    vnew = vnew_scr[...]
    knew_t = knew.T
    vnew_t = vnew.T

    head_of_lane = lax.broadcasted_iota(jnp.int32, (SW_HEADS, SW_WIDTH), 1) >> (SW_HD.bit_length() - 1)
    head_mask = head_of_lane == lax.broadcasted_iota(jnp.int32, (SW_HEADS, SW_WIDTH), 0)
    qbd = jnp.concatenate(
        [jnp.where(head_mask, jnp.broadcast_to(qkv[t:t + 1, 0:SW_WIDTH], (SW_HEADS, SW_WIDTH)), 0.0)
         for t in range(T)], axis=0).astype(BF16)
    hm_rows = jnp.concatenate([head_mask] * T, axis=0)

    lane = lax.broadcasted_iota(jnp.int32, (SW_WIDTH, LANES), 1)
    shift = LANES - T
    scale = SW_HD ** -0.5

    s_cols = []
    k_roll = pltpu.roll(kt_ref[0, :, 0:LANES], shift, 1)
    v_roll = pltpu.roll(vt_ref[0, :, 0:LANES], shift, 1)
    for j in range(ncol):
        cols = slice(j * LANES, (j + 1) * LANES)
        s_cols.append(_dot(qbd, kt_ref[0, :, cols].astype(BF16)) * scale)
        if j + 1 < ncol:
            nxt = slice((j + 1) * LANES, (j + 2) * LANES)
            k_next = pltpu.roll(kt_ref[0, :, nxt], shift, 1)
            v_next = pltpu.roll(vt_ref[0, :, nxt], shift, 1)
        else:
            k_next = pltpu.roll(knew_t, shift, 1)
            v_next = pltpu.roll(vnew_t, shift, 1)
        kto_ref[0, :, cols] = jnp.where(lane < shift, k_roll, k_next)
        vto_ref[0, :, cols] = jnp.where(lane < shift, v_roll, v_next)
        k_roll, v_roll = k_next, v_next
    s_cols.append(_dot_nt(qbd, knew.astype(BF16)) * scale)
    s_all = jnp.concatenate(s_cols, axis=1)

    wtot = W + LANES
    j_idx = lax.broadcasted_iota(jnp.int32, (R, wtot), 1)
    t_idx = lax.broadcasted_iota(jnp.int32, (R, wtot), 0) >> (SW_HEADS.bit_length() - 1)
    dist = jnp.where(j_idx < W, W + t_idx - j_idx, t_idx - (j_idx - W))
    exists = (j_idx < W + T) & (dist >= 0)

    ms, ls, ps = [], [], []
    for win, d in dil:
        valid = exists & ((dist & (d - 1)) == 0) & (dist <= win)
        sb = jnp.where(valid, s_all, -jnp.inf)
        m = jnp.max(sb, axis=-1, keepdims=True)
        pe = jnp.exp(sb - m)
        ms.append(m)
        ls.append(jnp.sum(pe, axis=-1, keepdims=True))
        ps.append(pe.astype(BF16))
    p_all = jnp.concatenate(ps, axis=0)
    num = _dot(p_all[:, W:], vnew.astype(BF16))
    for j in range(ncol):
        cols = slice(j * LANES, (j + 1) * LANES)
        num = num + _dot_nt(p_all[:, cols], vt_ref[0, :, cols].astype(BF16))
    mx = jnp.maximum(jnp.maximum(ms[0], ms[1]), ms[2])
    wts = [jnp.exp(m - mx) for m in ms]
    numer = sum(wts[i] * num[i * R:(i + 1) * R] for i in range(len(dil)))
    denom = sum(wts[i] * ls[i] for i in range(len(dil)))
    o = jnp.where(hm_rows, numer / denom, 0.0)
    zb = zb_ref[0]
    gate = _silu(zb)
    for t in range(T):
        row = jnp.sum(o[t * SW_HEADS:(t + 1) * SW_HEADS], axis=0, keepdims=True)
        ob_ref[0, t:t + 1, :] = row * gate[t:t + 1]


def _sattn(qkvb, zb, kt, vt):
    B, T, _ = qkvb.shape
    W = kt.shape[-1]
    kern = functools.partial(_sattn_kernel, T=T, W=W, dil=DILATIONS)
    small = lambda c: pl.BlockSpec((1, T, c), lambda b: (b, 0, 0))
    big = pl.BlockSpec((1, SW_WIDTH, W), lambda b: (b, 0, 0))
    return pl.pallas_call(
        kern,
        grid=(B,),
        in_specs=[small(3 * SW_WIDTH), small(SW_WIDTH), big, big],
        out_specs=[small(SW_WIDTH), big, big],
        out_shape=[jax.ShapeDtypeStruct((B, T, SW_WIDTH), F32),
                   jax.ShapeDtypeStruct((B, SW_WIDTH, W), F32),
                   jax.ShapeDtypeStruct((B, SW_WIDTH, W), F32)],
        scratch_shapes=[pltpu.VMEM((LANES, SW_WIDTH), F32), pltpu.VMEM((LANES, SW_WIDTH), F32)],
        compiler_params=pltpu.CompilerParams(
            dimension_semantics=("arbitrary",), vmem_limit_bytes=VMEM_LIMIT),
        name="sattn",
    )(qkvb, zb, kt, vt)


def _out_kernel(x_ref, oa_ref, *rest, nbranch):
    if nbranch:
        parts = rest[:2 * nbranch]
        zb_ref, w_ref, fg_ref, y_ref = rest[2 * nbranch:]
        slabs = []
        for p in range(SW_WIDTH // LANES):
            lses = [parts[2 * i + 1][0, p] for i in range(nbranch)]
            mx = functools.reduce(jnp.maximum, lses)
            wts = [jnp.exp(l - mx) for l in lses]
            numer = sum(wts[i] * parts[2 * i][0, p] for i in range(nbranch))
            slabs.append(numer / sum(wts))
        ob = jnp.concatenate(slabs, axis=-1) * _silu(zb_ref[0])
    else:
        ob_ref, w_ref, fg_ref, y_ref = rest
        ob = ob_ref[0]
    acc = _dot(oa_ref[0].astype(BF16), w_ref[0:DN_WIDTH, :])
    acc = acc + _dot(ob.astype(BF16), w_ref[DN_WIDTH:, :])
    y = x_ref[0] + acc
    y_ref[0] = (y * lax.rsqrt(jnp.mean(y * y, axis=-1, keepdims=True) + NORM_EPS)) * fg_ref[...]


def _out(x, oa, parts, zb, ob, w, fg, *, tm, name):
    B, L, _ = x.shape
    nbranch = len(parts) // 2
    row = lambda b, t: (b, t, 0)
    const2 = lambda b, t: (0, 0)
    npair = SW_WIDTH // LANES
    in_specs = [pl.BlockSpec((1, tm, D_MODEL), row), pl.BlockSpec((1, tm, DN_WIDTH), row)]
    args = [x, oa]
    if nbranch:
        in_specs += [pl.BlockSpec((1, npair, tm, LANES), lambda b, t: (b, 0, t, 0))] * (2 * nbranch)
        in_specs.append(pl.BlockSpec((1, tm, SW_WIDTH), row))
        args += list(parts) + [zb]
    else:
        in_specs.append(pl.BlockSpec((1, tm, SW_WIDTH), row))
        args.append(ob)
    in_specs += [pl.BlockSpec((DN_WIDTH + SW_WIDTH, D_MODEL), const2), pl.BlockSpec((1, D_MODEL), const2)]
    args += [w, fg]
    return pl.pallas_call(
        functools.partial(_out_kernel, nbranch=nbranch),
        grid=(B, L // tm),
        in_specs=in_specs,
        out_specs=pl.BlockSpec((1, tm, D_MODEL), row),
        out_shape=jax.ShapeDtypeStruct((B, L, D_MODEL), F32),
        compiler_params=pltpu.CompilerParams(
            dimension_semantics=("arbitrary", "arbitrary"), vmem_limit_bytes=VMEM_LIMIT),
        name=name,
    )(*args)


def _pad_lanes(v):
    return jnp.pad(v.astype(F32), (0, LANES - v.shape[0])).reshape(1, LANES)


def _tile(n, pref):
    return pref if n % pref == 0 else n


def _layer_prompt(x, wp, g_row, conv_w, alog_row, dtb_row, ng_row, w_out16, fg_row):
    B, L, _ = x.shape
    keep = min(WIN_MAX, L)
    dils = tuple(d for _, d in DILATIONS)
    tm = _tile(L, 512)
    raw, za, zb, ab, qb1, qb4, qb16, kv_last = _proj(x, g_row, wp, dils=dils, keep=keep, tm=tm)
    halo0 = jnp.zeros((B, HALO, CONV_CH), F32)
    s0 = jnp.zeros((B, DN_HEADS, DN_DK, DN_DV), F32)
    oa, s_new = _gdn(raw, za, ab, conv_w, alog_row, dtb_row, ng_row, halo0, s0,
                     C=DN_CHUNK, TL=tm, valid_len=tm, name="gdn_prompt")
    parts = []
    for qb, (win, d) in zip((qb1, qb4, qb16), DILATIONS):
        parts += _swa(qb, d=d, steps=win // d)
    y = _out(x, oa, parts, zb, None, w_out16, fg_row, tm=tm, name="out_prompt")
    new_conv = raw[:, L - (CONV_W - 1):, :]
    new_k = kv_last[:, :, :SW_WIDTH].reshape(B, keep, SW_HEADS, SW_HD)
    new_v = kv_last[:, :, SW_WIDTH:].reshape(B, keep, SW_HEADS, SW_HD)
    return y, new_conv, s_new, new_k, new_v


def _layer_sample(x, conv_state, s0, k_win, v_win, wp, g_row, conv_w, alog_row, dtb_row, ng_row,
                  w_out16, fg_row):
    B, T, _ = x.shape
    W = k_win.shape[1]
    rows = x.reshape(1, B * T, D_MODEL)
    raw, za, zb, ab, qkvb = _proj(rows, g_row, wp, dils=(), keep=0, tm=_tile(B * T, 256))
    raw = raw.reshape(B, T, CONV_CH)
    tp = SUBLANES
    padt = lambda a: jnp.pad(a.reshape(B, T, -1), ((0, 0), (0, tp - T), (0, 0)))
    halo0 = jnp.pad(conv_state, ((0, 0), (HALO - (CONV_W - 1), 0), (0, 0)))
    oa, s_new = _gdn(padt(raw), padt(za), padt(ab), conv_w, alog_row, dtb_row, ng_row, halo0, s0,
                     C=tp, TL=tp, valid_len=T, name="gdn_sample")
    oa = oa[:, :T].reshape(1, B * T, DN_WIDTH)
    kt = jnp.transpose(k_win, (0, 2, 3, 1)).reshape(B, SW_WIDTH, W)
    vt = jnp.transpose(v_win, (0, 2, 3, 1)).reshape(B, SW_WIDTH, W)
    ob, kt_new, vt_new = _sattn(qkvb.reshape(B, T, 3 * SW_WIDTH), zb.reshape(B, T, SW_WIDTH), kt, vt)
    y = _out(rows, oa, (), None, ob.reshape(1, B * T, SW_WIDTH), w_out16, fg_row,
             tm=_tile(B * T, 256), name="out_sample")
    new_conv = jnp.concatenate([conv_state, raw], axis=1)[:, -(CONV_W - 1):]
    new_k = jnp.transpose(kt_new.reshape(B, SW_HEADS, SW_HD, W), (0, 3, 1, 2))
    new_v = jnp.transpose(vt_new.reshape(B, SW_HEADS, SW_HD, W), (0, 3, 1, 2))
    return y.reshape(B, T, D_MODEL), new_conv, s_new, new_k, new_v


def kernel(x_prompt, x_sample, state_conv, state_delta, cache_k_win, cache_v_win, norm_g, final_norm_g,
           w_in, conv_w, a_log, dt_bias, dn_norm_g, w_out):
    assert norm_g.shape[0] == 1, "single layer"
    w = w_in[0]
    o1 = CONV_CH + DN_WIDTH
    o3 = o1 + 2 * DN_HEADS
    o4 = o3 + 3 * SW_WIDTH
    wp = jnp.concatenate(
        [w[:, :o1], w[:, o3:], w[:, o1:o3], jnp.zeros((D_MODEL, LANES - 2 * DN_HEADS), w.dtype)],
        axis=1).astype(BF16)
    assert wp.shape[1] == W_COLS and o4 - o3 == 3 * SW_WIDTH
    g_row = norm_g[0].reshape(1, D_MODEL)
    fg_row = final_norm_g.reshape(1, D_MODEL)
    alog_row = _pad_lanes(a_log[0])
    dtb_row = _pad_lanes(dt_bias[0])
    ng_row = dn_norm_g[0].reshape(1, DN_DV)
    w_out16 = w_out[0].astype(BF16)
    cw = conv_w[0]
    common = (wp, g_row, cw, alog_row, dtb_row, ng_row, w_out16, fg_row)
    yp, pc, ps, pk, pv = _layer_prompt(x_prompt, *common)
    ys, sc, ss, sk, sv = _layer_sample(x_sample, state_conv[0], state_delta[0], cache_k_win[0],
                                       cache_v_win[0], *common)
    return (yp, ys, pc[None], ps[None], pk[None], pv[None], sc[None], ss[None], sk[None], sv[None])
```

```python
import functools

import numpy as np
import jax
import jax.numpy as jnp
from jax import lax
from jax.experimental import pallas as pl
from jax.experimental.pallas import tpu as pltpu

F32 = jnp.float32
BF16 = jnp.bfloat16

D_MODEL = 1024
DN_HEADS = 4
DN_DK = 128
DN_DV = 128
QK_W = DN_HEADS * DN_DK
DN_WIDTH = DN_HEADS * DN_DV
CONV_W = 4
CONV_CH = 2 * QK_W + DN_WIDTH
DN_CHUNK = 64
SW_HEADS = 8
SW_HD = 64
SW_WIDTH = SW_HEADS * SW_HD
DILATIONS = ((128, 1), (512, 4), (2048, 16))
WIN_MAX = 2048
NORM_EPS = 1e-6

LANES = 128
SUBLANES = 8
VMEM_LIMIT = 56 * 1024 * 1024

C_RAW = 0
C_ZA = CONV_CH
C_QKVB = C_ZA + DN_WIDTH
C_ZB = C_QKVB + 3 * SW_WIDTH
C_AB = C_ZB + SW_WIDTH
W_COLS = C_AB + LANES
HALO = SUBLANES
GDN_TILE = DN_HEADS * DN_CHUNK


def _dot(a, b):
    return lax.dot_general(a, b, (((a.ndim - 1,), (0,)), ((), ())), preferred_element_type=F32)


def _dot_nt(a, b):
    return lax.dot_general(a, b, (((1,), (1,)), ((), ())), preferred_element_type=F32)


def _dot_tn(a, b):
    return lax.dot_general(a, b, (((0,), (0,)), ((), ())), preferred_element_type=F32)


def _bdot(a, b):
    return lax.dot_general(a, b, (((2,), (1,)), ((0,), (0,))), preferred_element_type=F32)


def _bdot_tn(a, b):
    return lax.dot_general(a, b, (((1,), (1,)), ((0,), (0,))), preferred_element_type=F32)


def _dot_exact_lhs(a16, x):
    hi = x.astype(BF16)
    r1 = x - hi.astype(F32)
    mid = r1.astype(BF16)
    lo = (r1 - mid.astype(F32)).astype(BF16)
    return _dot(a16, lo) + _dot(a16, mid) + _dot(a16, hi)


def _sigmoid(x):
    return 1.0 / (1.0 + jnp.exp(-x))


def _silu(x):
    return x * _sigmoid(x)


def _conv_norm_gates(ext_scr, cw_ref, alog_ref, dtb_ref, ab, qkvn_ref, gb_ref, rows, live):
    cw = cw_ref[...]
    y = cw[0:1] * ext_scr[HALO - 3:HALO - 3 + rows, :]
    y = y + cw[1:2] * ext_scr[HALO - 2:HALO - 2 + rows, :]
    y = y + cw[2:3] * ext_scr[HALO - 1:HALO - 1 + rows, :]
    y = y + cw[3:4] * ext_scr[HALO:HALO + rows, :]
    qkv = _silu(y)
    for h in range(DN_HEADS):
        sl = slice(h * DN_DK, (h + 1) * DN_DK)
        qh = qkv[:, sl]
        qkvn_ref[0, :, sl] = qh * lax.rsqrt(jnp.sum(qh * qh, -1, keepdims=True) + NORM_EPS) * (DN_DK ** -0.5)
        sl = slice(QK_W + h * DN_DK, QK_W + (h + 1) * DN_DK)
        kh = qkv[:, sl]
        qkvn_ref[0, :, sl] = kh * lax.rsqrt(jnp.sum(kh * kh, -1, keepdims=True) + NORM_EPS)
    qkvn_ref[0, :, 2 * QK_W:] = qkv[:, 2 * QK_W:]
    xg = ab + dtb_ref[...]
    softplus = jnp.maximum(xg, 0.0) + jnp.log1p(jnp.exp(-jnp.abs(xg)))
    g = -jnp.exp(alog_ref[...]) * softplus
    lane = lax.broadcasted_iota(jnp.int32, (rows, LANES), 1)
    gb = jnp.where(lane < DN_HEADS, g, _sigmoid(ab))
    if live is not None:
        gb = jnp.where(live, gb, 0.0)
    gb_ref[0] = gb


def _normed(x_ref, g_ref):
    x = x_ref[0]
    ms = jnp.mean(x * x, axis=-1, keepdims=True)
    return ((x * lax.rsqrt(ms + NORM_EPS)) * g_ref[...]).astype(BF16)


def _proj_sample_kernel(x_ref, g_ref, w_ref, raw_ref, za_ref, zb_ref, ab_ref, qb_ref):
    h = _normed(x_ref, g_ref)
    raw_ref[0] = _dot(h, w_ref[:, C_RAW:C_ZA])
    za_ref[0] = _dot(h, w_ref[:, C_ZA:C_QKVB])
    zb_ref[0] = _dot(h, w_ref[:, C_ZB:C_AB])
    ab_ref[0] = _dot(h, w_ref[:, C_AB:W_COLS])
    qb_ref[0] = _dot(h, w_ref[:, C_QKVB:C_ZB])


def _proj_prompt_kernel(x_ref, g_ref, w_ref, cw_ref, alog_ref, dtb_ref,
                        qkvn_ref, gb_ref, za_ref, zb_ref, rawlast_ref, kv_ref, *rest, dils, tm):
    qb_refs = rest[:len(dils)]
    slab_scr, ext_scr = rest[len(dils):]

    @pl.when(pl.program_id(1) == 0)
    def _():
        ext_scr[0:HALO, :] = jnp.zeros((HALO, CONV_CH), F32)

    h = _normed(x_ref, g_ref)
    ext_scr[HALO:HALO + tm, :] = _dot(h, w_ref[:, C_RAW:C_ZA])
    za_ref[0] = _dot(h, w_ref[:, C_ZA:C_QKVB])
    zb_ref[0] = _dot(h, w_ref[:, C_ZB:C_AB])
    ab = _dot(h, w_ref[:, C_AB:W_COLS])
    _conv_norm_gates(ext_scr, cw_ref, alog_ref, dtb_ref, ab, qkvn_ref, gb_ref, tm, None)
    last = ext_scr[tm:tm + HALO, :]
    rawlast_ref[0] = last
    ext_scr[0:HALO, :] = last

    qkvb = _dot(h, w_ref[:, C_QKVB:C_ZB])
    kv_ref[0] = qkvb[:, SW_WIDTH:]
    nslab = 3 * SW_WIDTH // LANES
    for j in range(nslab):
        slab_scr[j] = qkvb[:, j * LANES:(j + 1) * LANES]
    for qb_ref, d in zip(qb_refs, dils):
        if d == 1:
            qb_ref[0, 0] = qkvb.astype(BF16)
            continue
        for r in range(d):
            for j in range(nslab):
                qb_ref[0, r, :, j * LANES:(j + 1) * LANES] = (
                    slab_scr[j, pl.ds(r, tm // d, stride=d), :].astype(BF16))


def _const_spec(shape):
    zeros = (0,) * len(shape)
    return pl.BlockSpec(shape, lambda *_: zeros)


def _proj_prompt(x, g, w, cw, alog_row, dtb_row, *, dils, keep, tm):
    B, L, _ = x.shape
    row = lambda b, t: (b, t, 0)
    first = (L - keep) // tm
    out_shape = [jax.ShapeDtypeStruct((B, L, CONV_CH), F32),
                 jax.ShapeDtypeStruct((B, L, LANES), F32),
                 jax.ShapeDtypeStruct((B, L, DN_WIDTH), F32),
                 jax.ShapeDtypeStruct((B, L, SW_WIDTH), F32),
                 jax.ShapeDtypeStruct((B, HALO, CONV_CH), F32),
                 jax.ShapeDtypeStruct((B, keep, 2 * SW_WIDTH), F32)]
    out_specs = [pl.BlockSpec((1, tm, CONV_CH), row),
                 pl.BlockSpec((1, tm, LANES), row),
                 pl.BlockSpec((1, tm, DN_WIDTH), row),
                 pl.BlockSpec((1, tm, SW_WIDTH), row),
                 pl.BlockSpec((1, HALO, CONV_CH), lambda b, t: (b, 0, 0)),
                 pl.BlockSpec((1, tm, 2 * SW_WIDTH), lambda b, t: (b, jnp.maximum(t - first, 0), 0))]
    for d in dils:
        out_shape.append(jax.ShapeDtypeStruct((B, d, L // d, 3 * SW_WIDTH), BF16))
        out_specs.append(pl.BlockSpec((1, d, tm // d, 3 * SW_WIDTH), lambda b, t: (b, 0, t, 0)))
    return pl.pallas_call(
        functools.partial(_proj_prompt_kernel, dils=dils, tm=tm),
        grid=(B, L // tm),
        in_specs=[pl.BlockSpec((1, tm, D_MODEL), row),
                  _const_spec((1, D_MODEL)),
                  pl.BlockSpec((D_MODEL, W_COLS), lambda b, t: (0, 0), pipeline_mode=pl.Buffered(1)),
                  _const_spec((CONV_W, CONV_CH)),
                  _const_spec((1, LANES)),
                  _const_spec((1, LANES))],
        out_specs=out_specs,
        out_shape=out_shape,
        scratch_shapes=[pltpu.VMEM((3 * SW_WIDTH // LANES, tm, LANES), F32),
                        pltpu.VMEM((tm + HALO, CONV_CH), F32)],
        compiler_params=pltpu.CompilerParams(
            dimension_semantics=("arbitrary", "arbitrary"), vmem_limit_bytes=VMEM_LIMIT),
        name="proj_prompt",
    )(x, g, w, cw, alog_row, dtb_row)


def _proj_sample(x, g, w, *, tm):
    B, L, _ = x.shape
    row = lambda b, t: (b, t, 0)
    widths = (CONV_CH, DN_WIDTH, SW_WIDTH, LANES, 3 * SW_WIDTH)
    return pl.pallas_call(
        _proj_sample_kernel,
        grid=(B, L // tm),
        in_specs=[pl.BlockSpec((1, tm, D_MODEL), row), _const_spec((1, D_MODEL)),
                  _const_spec((D_MODEL, W_COLS))],
        out_specs=[pl.BlockSpec((1, tm, c), row) for c in widths],
        out_shape=[jax.ShapeDtypeStruct((B, L, c), F32) for c in widths],
        compiler_params=pltpu.CompilerParams(
            dimension_semantics=("arbitrary", "arbitrary"), vmem_limit_bytes=VMEM_LIMIT),
        name="proj_sample",
    )(x, g, w)


def _gdn_pre_kernel(ext_ref, ab_ref, cw_ref, alog_ref, dtb_ref, qkvn_ref, gb_ref, ext_scr, *, rows, group, t_new):
    ext_scr[0:HALO, :] = jnp.zeros((HALO, CONV_CH), F32)
    ext_scr[HALO:HALO + rows, :] = ext_ref[0]
    live = (lax.broadcasted_iota(jnp.int32, (rows, LANES), 0) & (group - 1)) >= group - t_new
    _conv_norm_gates(ext_scr, cw_ref, alog_ref, dtb_ref, ab_ref[0], qkvn_ref, gb_ref, rows, live)


def _gdn_pre(ext, ab, cw, alog_row, dtb_row, *, group, t_new):
    _, rows, _ = ext.shape
    return pl.pallas_call(
        functools.partial(_gdn_pre_kernel, rows=rows, group=group, t_new=t_new),
        grid=(1,),
        in_specs=[_const_spec((1, rows, CONV_CH)), _const_spec((1, rows, LANES)),
                  _const_spec((CONV_W, CONV_CH)), _const_spec((1, LANES)), _const_spec((1, LANES))],
        out_specs=[_const_spec((1, rows, CONV_CH)), _const_spec((1, rows, LANES))],
        out_shape=[jax.ShapeDtypeStruct((1, rows, CONV_CH), F32), jax.ShapeDtypeStruct((1, rows, LANES), F32)],
        scratch_shapes=[pltpu.VMEM((rows + HALO, CONV_CH), F32)],
        compiler_params=pltpu.CompilerParams(
            dimension_semantics=("arbitrary",), vmem_limit_bytes=VMEM_LIMIT),
        name="gdn_pre_sample",
    )(ext, ab, cw, alog_row, dtb_row)


def _block_masks(n, c):
    r = np.arange(n)[:, None]
    cc = np.arange(n)[None, :]
    x = r ^ cc
    same = x < c
    incl = same & (cc <= r)
    levels = [(x >> 1) == 0]
    k = 1
    while (2 << k) <= c:
        levels.append((x >> k) == 1)
        k += 1
    masks = np.stack([r == cc, incl] + levels).astype(np.float32)
    lmask = np.concatenate([incl, same], axis=0).astype(np.float32)
    return jnp.asarray(masks), jnp.asarray(lmask, dtype=BF16)


def _gdn_tiles(tiles, masks_ref, lmask_ref):
    n = tiles[0][0].shape[0]
    nt = len(tiles)
    nlev = masks_ref.shape[0] - 2
    eye = masks_ref[0]
    incl = masks_ref[1]
    g_b = [jnp.broadcast_to(t[3], (n, LANES)) for t in tiles]
    beta_b = [jnp.broadcast_to(t[4], (n, LANES)) for t in tiles]
    cs = _dot_exact_lhs(lmask_ref[...], jnp.concatenate(g_b, axis=1))
    gc_b = [cs[:n, i * LANES:(i + 1) * LANES] for i in range(nt)]
    gl_b = [cs[n:, i * LANES:(i + 1) * LANES] for i in range(nt)]
    kb = [tiles[i][1] * beta_b[i] for i in range(nt)]
    k16 = [t[1].astype(BF16) for t in tiles]
    kk = [_dot_nt(kb[i].astype(BF16), k16[i]) for i in range(nt)]
    qk = [_dot_nt(tiles[i][0].astype(BF16), k16[i]) for i in range(nt)]
    a, attn = [], []
    for i in range(nt):
        gc_sq = jnp.concatenate([gc_b[i]] * (n // LANES), axis=1)
        gc_row = jnp.sum(gc_sq * eye, axis=0, keepdims=True)
        dec = jnp.exp(jnp.minimum(gc_sq - gc_row, 0.0)) * incl
        a.append(kk[i] * (dec - eye))
        attn.append(qk[i] * dec)
    t = [eye - a[i] * masks_ref[2] for i in range(nt)]
    for lv in range(1, nlev):
        t16 = [x.astype(BF16) for x in t]
        p = [_dot(t16[i], (a[i] * masks_ref[2 + lv]).astype(BF16)) for i in range(nt)]
        t = [t[i] - _dot(p[i].astype(BF16), t16[i]) for i in range(nt)]
    e_gc = [jnp.exp(x) for x in gc_b]
    sol = [_dot(t[i].astype(BF16),
                jnp.concatenate([tiles[i][2] * beta_b[i], kb[i] * e_gc[i]], axis=1).astype(BF16))
           for i in range(nt)]
    return [(sol[i][:, :DN_DV], sol[i][:, DN_DV:], tiles[i][0] * e_gc[i],
             tiles[i][1] * jnp.exp(gl_b[i] - gc_b[i]), attn[i], gl_b[i]) for i in range(nt)]


def _out_norm(o, ng, za):
    return ((o * lax.rsqrt(jnp.mean(o * o, -1, keepdims=True) + NORM_EPS)) * ng) * _silu(za)


def _gdn_prompt_kernel(qkvn_ref, gb_ref, za_ref, ng_ref, masks_ref, lmask_ref, o_ref, s_out_ref, s_scr,
                       *, C, TL, G):
    H = DN_HEADS

    @pl.when(pl.program_id(1) == 0)
    def _():
        s_scr[...] = jnp.zeros_like(s_scr)

    ng = ng_ref[...]

    def group(gi, carry):
        base = pl.multiple_of(gi * (G * C), G * C)
        rows = [pl.ds(base + i * C, C) for i in range(G)]

        def stack(ref, rws, off):
            return jnp.concatenate(
                [ref[0, rws, off + h * LANES:off + (h + 1) * LANES] for h in range(H)], axis=0)

        tiles, zas = [], []
        for rws in rows:
            gbc = gb_ref[0, rws, :]
            g_col = jnp.concatenate([gbc[:, h:h + 1] for h in range(H)], axis=0)
            beta_col = jnp.concatenate([gbc[:, H + h:H + h + 1] for h in range(H)], axis=0)
            zas.append(stack(za_ref, rws, 0))
            tiles.append((stack(qkvn_ref, rws, 0), stack(qkvn_ref, rws, QK_W),
                          stack(qkvn_ref, rws, 2 * QK_W), g_col, beta_col))
        tiles = _gdn_tiles(tiles, masks_ref, lmask_ref)
        s = [s_scr[h] for h in range(H)]
        outs = []
        for (u, w, q_dec, k_dec, attn, gl_b), za in zip(tiles, zas):
            ws, qs = [], []
            for h in range(H):
                blk = slice(h * C, (h + 1) * C)
                r = _dot(jnp.concatenate([w[blk], q_dec[blk]], axis=0).astype(BF16), s[h].astype(BF16))
                ws.append(r[:C])
                qs.append(r[C:])
            v_new = u - jnp.concatenate(ws, axis=0)
            vn16 = v_new.astype(BF16)
            o = jnp.concatenate(qs, axis=0) + _dot(attn.astype(BF16), vn16)
            for h in range(H):
                blk = slice(h * C, (h + 1) * C)
                gl_s = jnp.broadcast_to(jnp.exp(gl_b[h * C:h * C + 1, :]), (DN_DK, DN_DV))
                s[h] = s[h] * gl_s + _dot_tn(k_dec[blk].astype(BF16), vn16[blk])
            outs.append(_out_norm(o, ng, za))
        for rws, out in zip(rows, outs):
            for h in range(H):
                o_ref[0, rws, h * DN_DV:(h + 1) * DN_DV] = out[h * C:(h + 1) * C].astype(o_ref.dtype)
        for h in range(H):
            s_scr[h] = s[h]
        return carry

    lax.fori_loop(0, TL // (G * C), group, 0)
    s_out_ref[0] = s_scr[...]


def _gdn_prompt(qkvn, gb, za, ng_row, *, C, TL):
    B, L, _ = qkvn.shape
    n = DN_HEADS * C
    masks, lmask = _block_masks(n, C)
    row = lambda b, t: (b, t, 0)
    return pl.pallas_call(
        functools.partial(_gdn_prompt_kernel, C=C, TL=TL, G=4 if TL % (4 * C) == 0 else 1),
        grid=(B, L // TL),
        in_specs=[pl.BlockSpec((1, TL, CONV_CH), row),
                  pl.BlockSpec((1, TL, LANES), row),
                  pl.BlockSpec((1, TL, DN_WIDTH), row),
                  _const_spec((1, DN_DV)), _const_spec(masks.shape), _const_spec(lmask.shape)],
        out_specs=[pl.BlockSpec((1, TL, DN_WIDTH), row),
                   pl.BlockSpec((1, DN_HEADS, DN_DK, DN_DV), lambda b, t: (b, 0, 0, 0))],
        out_shape=[jax.ShapeDtypeStruct((B, L, DN_WIDTH), BF16),
                   jax.ShapeDtypeStruct((B, DN_HEADS, DN_DK, DN_DV), F32)],
        scratch_shapes=[pltpu.VMEM((DN_HEADS, DN_DK, DN_DV), F32)],
        compiler_params=pltpu.CompilerParams(
            dimension_semantics=("arbitrary", "arbitrary"), vmem_limit_bytes=VMEM_LIMIT),
        name="gdn_prompt",
    )(qkvn, gb, za, ng_row, masks, lmask)


def _gdn_sample_kernel(qkvn_ref, gb_ref, za_ref, s0_ref, ng_ref, masks_ref, lmask_ref, o_ref, s_out_ref, *, C):
    n = qkvn_ref.shape[1]
    nb = n // C
    ng = ng_ref[...]
    split = lambda a: a.reshape(nb, C, a.shape[-1])
    tiles = _gdn_tiles(
        [(qkvn_ref[0, :, h * LANES:(h + 1) * LANES],
          qkvn_ref[0, :, QK_W + h * LANES:QK_W + (h + 1) * LANES],
          qkvn_ref[0, :, 2 * QK_W + h * LANES:2 * QK_W + (h + 1) * LANES],
          gb_ref[0, :, h:h + 1], gb_ref[0, :, DN_HEADS + h:DN_HEADS + h + 1]) for h in range(DN_HEADS)],
        masks_ref, lmask_ref)
    for h in range(DN_HEADS):
        sl = slice(h * LANES, (h + 1) * LANES)
        u, w, q_dec, k_dec, attn, gl_b = tiles[h]
        s0 = s0_ref[:, h]
        r = _bdot(jnp.concatenate([split(w), split(q_dec)], axis=1).astype(BF16), s0.astype(BF16))
        v_new = split(u) - r[:, :C]
        vn16 = v_new.astype(BF16)
        o = r[:, C:].reshape(n, DN_DV) + _dot(attn.astype(BF16), vn16.reshape(n, DN_DV))
        gl = split(jnp.exp(gl_b))[:, 0:1, :]
        s_out_ref[:, h] = s0 * gl + _bdot_tn(split(k_dec).astype(BF16), vn16)
        o_ref[0, :, sl] = _out_norm(o, ng, za_ref[0, :, sl]).astype(o_ref.dtype)


def _gdn_sample(qkvn, gb, za, s0, ng_row, *, C, n):
    _, rows, _ = qkvn.shape
    nb = n // C
    masks, lmask = _block_masks(n, C)
    row = lambda i: (0, i, 0)
    state = pl.BlockSpec((nb, DN_HEADS, DN_DK, DN_DV), lambda i: (i, 0, 0, 0))
    return pl.pallas_call(
        functools.partial(_gdn_sample_kernel, C=C),
        grid=(rows // n,),
        in_specs=[pl.BlockSpec((1, n, CONV_CH), row), pl.BlockSpec((1, n, LANES), row),
                  pl.BlockSpec((1, n, DN_WIDTH), row), state,
                  _const_spec((1, DN_DV)), _const_spec(masks.shape), _const_spec(lmask.shape)],
        out_specs=[pl.BlockSpec((1, n, DN_WIDTH), row), state],
        out_shape=[jax.ShapeDtypeStruct((1, rows, DN_WIDTH), BF16), jax.ShapeDtypeStruct(s0.shape, F32)],
        compiler_params=pltpu.CompilerParams(
            dimension_semantics=("arbitrary",), vmem_limit_bytes=VMEM_LIMIT),
        name="gdn_sample",
    )(qkvn, gb, za, s0, ng_row, masks, lmask)


def _swa_kernel(q_ref, kp_ref, kc_ref, vp_ref, vc_ref, o_ref, lse_ref, *, d, blk):
    n = pl.program_id(1)
    r = pl.program_id(2)
    qi = lax.broadcasted_iota(jnp.int32, (blk, 2 * blk), 0)
    kj = lax.broadcasted_iota(jnp.int32, (blk, 2 * blk), 1)
    mask = (kj >= qi) & (kj <= qi + blk) & ((kj >= blk) | (n > 0))
    lane = lax.broadcasted_iota(jnp.int32, (blk, LANES), 1)
    lo_half = lane < SW_HD
    half_masks = (jnp.where(lo_half, 1.0, 0.0).astype(BF16), jnp.where(lo_half, 0.0, 1.0).astype(BF16))
    rows = pl.ds(r, blk, stride=d) if d > 1 else pl.ds(0, blk)
    npair = SW_WIDTH // LANES
    scores, vvs = [], []
    for p in range(npair):
        sl = slice(p * LANES, (p + 1) * LANES)
        qp = q_ref[0, 0, :, sl]
        kk = jnp.concatenate([kp_ref[0, 0, :, sl], kc_ref[0, 0, :, sl]], axis=0)
        vvs.append(jnp.concatenate([vp_ref[0, 0, :, sl], vc_ref[0, 0, :, sl]], axis=0))
        for half in range(2):
            scores.append(_dot_nt(qp * half_masks[half], kk))
    probs, inv_l, lses = [], [], []
    for s in scores:
        s = jnp.where(mask, s * (SW_HD ** -0.5), -jnp.inf)
        m = jnp.max(s, axis=-1, keepdims=True)
        pe = jnp.exp(s - m)
        l = jnp.sum(pe, axis=-1, keepdims=True)
        probs.append(pe.astype(BF16))
        inv_l.append(1.0 / l)
        lses.append(m + jnp.log(l))
    pvs = [_dot(probs[i], vvs[i // 2]) for i in range(2 * npair)]
    for p in range(npair):
        o_ref[0, p, rows, :] = jnp.where(lo_half, pvs[2 * p] * inv_l[2 * p], pvs[2 * p + 1] * inv_l[2 * p + 1])
        lse_ref[0, p, rows, :] = jnp.where(lo_half, jnp.broadcast_to(lses[2 * p], (blk, LANES)),
                                           jnp.broadcast_to(lses[2 * p + 1], (blk, LANES)))


def _swa(qb, *, d, steps):
    B, _, n, _ = qb.shape
    blk = steps
    nb = n // blk
    L = n * d
    kern = functools.partial(_swa_kernel, d=d, blk=blk)
    cur = lambda c: (lambda b, i, r: (b, r, i, c))
    prev = lambda c: (lambda b, i, r: (b, r, jnp.maximum(i - 1, 0), c))
    blkspec = lambda f: pl.BlockSpec((1, 1, blk, SW_WIDTH), f)
    npair = SW_WIDTH // LANES
    out_spec = pl.BlockSpec((1, npair, blk * d, LANES), lambda b, i, r: (b, 0, i, 0))
    return pl.pallas_call(
        kern,
        grid=(B, nb, d),
        in_specs=[blkspec(cur(0)), blkspec(prev(1)), blkspec(cur(1)), blkspec(prev(2)), blkspec(cur(2))],
        out_specs=[out_spec, out_spec],
        out_shape=[jax.ShapeDtypeStruct((B, npair, L, LANES), F32)] * 2,
        compiler_params=pltpu.CompilerParams(
            dimension_semantics=("arbitrary", "arbitrary", "arbitrary"), vmem_limit_bytes=VMEM_LIMIT),
        name="swa_d%d" % d,
    )(qb, qb, qb, qb, qb)


def _sattn_kernel(qkv_ref, zb_ref, kt_ref, vt_ref, ob_ref, kto_ref, vto_ref, knew_scr, vnew_scr,
                  *, T, W, dil):
    R = T * SW_HEADS
    ncol = W // LANES

    @pl.when(pl.program_id(0) == 0)
    def _():
        knew_scr[...] = jnp.zeros_like(knew_scr)
        vnew_scr[...] = jnp.zeros_like(vnew_scr)

    qkv = qkv_ref[0]
    knew_scr[0:T, :] = qkv[:, SW_WIDTH:2 * SW_WIDTH]
    vnew_scr[0:T, :] = qkv[:, 2 * SW_WIDTH:]
    knew = knew_scr[...]
    vnew = vnew_scr[...]
    knew_t = knew.T
    vnew_t = vnew.T

    head_of_lane = lax.broadcasted_iota(jnp.int32, (SW_HEADS, SW_WIDTH), 1) >> (SW_HD.bit_length() - 1)
    head_mask = head_of_lane == lax.broadcasted_iota(jnp.int32, (SW_HEADS, SW_WIDTH), 0)
    qbd = jnp.concatenate(
        [jnp.where(head_mask, jnp.broadcast_to(qkv[t:t + 1, 0:SW_WIDTH], (SW_HEADS, SW_WIDTH)), 0.0)
         for t in range(T)], axis=0).astype(BF16)
    hm_rows = jnp.concatenate([head_mask] * T, axis=0)

    lane = lax.broadcasted_iota(jnp.int32, (SW_WIDTH, LANES), 1)
    shift = LANES - T
    scale = SW_HD ** -0.5

    s_cols = []
    k_roll = pltpu.roll(kt_ref[0, :, 0:LANES], shift, 1)
    v_roll = pltpu.roll(vt_ref[0, :, 0:LANES], shift, 1)
    for j in range(ncol):
        cols = slice(j * LANES, (j + 1) * LANES)
        s_cols.append(_dot(qbd, kt_ref[0, :, cols].astype(BF16)) * scale)
        if j + 1 < ncol:
            nxt = slice((j + 1) * LANES, (j + 2) * LANES)
            k_next = pltpu.roll(kt_ref[0, :, nxt], shift, 1)
            v_next = pltpu.roll(vt_ref[0, :, nxt], shift, 1)
        else:
            k_next = pltpu.roll(knew_t, shift, 1)
            v_next = pltpu.roll(vnew_t, shift, 1)
        kto_ref[0, :, cols] = jnp.where(lane < shift, k_roll, k_next)
        vto_ref[0, :, cols] = jnp.where(lane < shift, v_roll, v_next)
        k_roll, v_roll = k_next, v_next
    s_cols.append(_dot_nt(qbd, knew.astype(BF16)) * scale)
    s_all = jnp.concatenate(s_cols, axis=1)

    wtot = W + LANES
    j_idx = lax.broadcasted_iota(jnp.int32, (R, wtot), 1)
    t_idx = lax.broadcasted_iota(jnp.int32, (R, wtot), 0) >> (SW_HEADS.bit_length() - 1)
    dist = jnp.where(j_idx < W, W + t_idx - j_idx, t_idx - (j_idx - W))
    exists = (j_idx < W + T) & (dist >= 0)

    ms, ls, ps = [], [], []
    for win, d in dil:
        valid = exists & ((dist & (d - 1)) == 0) & (dist <= win)
        sb = jnp.where(valid, s_all, -jnp.inf)
        m = jnp.max(sb, axis=-1, keepdims=True)
        pe = jnp.exp(sb - m)
        ms.append(m)
        ls.append(jnp.sum(pe, axis=-1, keepdims=True))
        ps.append(pe.astype(BF16))
    p_all = jnp.concatenate(ps, axis=0)
    num = _dot(p_all[:, W:], vnew.astype(BF16))
    for j in range(ncol):
        cols = slice(j * LANES, (j + 1) * LANES)
        num = num + _dot_nt(p_all[:, cols], vt_ref[0, :, cols].astype(BF16))
    mx = jnp.maximum(jnp.maximum(ms[0], ms[1]), ms[2])
    wts = [jnp.exp(m - mx) for m in ms]
    numer = sum(wts[i] * num[i * R:(i + 1) * R] for i in range(len(dil)))
    denom = sum(wts[i] * ls[i] for i in range(len(dil)))
    o = jnp.where(hm_rows, numer / denom, 0.0)
    zb = zb_ref[0]
    gate = _silu(zb)
    for t in range(T):
        row = jnp.sum(o[t * SW_HEADS:(t + 1) * SW_HEADS], axis=0, keepdims=True)
        ob_ref[0, t:t + 1, :] = row * gate[t:t + 1]


def _sattn(qkvb, zb, kt, vt):
    B, T, _ = qkvb.shape
    W = kt.shape[-1]
    kern = functools.partial(_sattn_kernel, T=T, W=W, dil=DILATIONS)
    small = lambda c: pl.BlockSpec((1, T, c), lambda b: (b, 0, 0))
    big = pl.BlockSpec((1, SW_WIDTH, W), lambda b: (b, 0, 0))
    return pl.pallas_call(
        kern,
        grid=(B,),
        in_specs=[small(3 * SW_WIDTH), small(SW_WIDTH), big, big],
        out_specs=[small(SW_WIDTH), big, big],
        out_shape=[jax.ShapeDtypeStruct((B, T, SW_WIDTH), F32),
                   jax.ShapeDtypeStruct((B, SW_WIDTH, W), F32),
                   jax.ShapeDtypeStruct((B, SW_WIDTH, W), F32)],
        scratch_shapes=[pltpu.VMEM((LANES, SW_WIDTH), F32), pltpu.VMEM((LANES, SW_WIDTH), F32)],
        compiler_params=pltpu.CompilerParams(
            dimension_semantics=("arbitrary",), vmem_limit_bytes=VMEM_LIMIT),
        name="sattn",
    )(qkvb, zb, kt, vt)


def _out_kernel(x_ref, oa_ref, *rest, nbranch):
    if nbranch:
        parts = rest[:2 * nbranch]
        zb_ref, w_ref, fg_ref, y_ref = rest[2 * nbranch:]
        slabs = []
        for p in range(SW_WIDTH // LANES):
            lses = [parts[2 * i + 1][0, p] for i in range(nbranch)]
            mx = functools.reduce(jnp.maximum, lses)
            wts = [jnp.exp(l - mx) for l in lses]
            numer = sum(wts[i] * parts[2 * i][0, p] for i in range(nbranch))
            slabs.append(numer / sum(wts))
        ob = jnp.concatenate(slabs, axis=-1) * _silu(zb_ref[0])
    else:
        ob_ref, w_ref, fg_ref, y_ref = rest
        ob = ob_ref[0]
    acc = _dot(oa_ref[0].astype(BF16), w_ref[0:DN_WIDTH, :])
    acc = acc + _dot(ob.astype(BF16), w_ref[DN_WIDTH:, :])
    y = x_ref[0] + acc
    y_ref[0] = (y * lax.rsqrt(jnp.mean(y * y, axis=-1, keepdims=True) + NORM_EPS)) * fg_ref[...]


def _out(x, oa, parts, zb, ob, w, fg, *, tm, name):
    B, L, _ = x.shape
    nbranch = len(parts) // 2
    row = lambda b, t: (b, t, 0)
    npair = SW_WIDTH // LANES
    in_specs = [pl.BlockSpec((1, tm, D_MODEL), row), pl.BlockSpec((1, tm, DN_WIDTH), row)]
    args = [x, oa]
    if nbranch:
        in_specs += [pl.BlockSpec((1, npair, tm, LANES), lambda b, t: (b, 0, t, 0))] * (2 * nbranch)
        in_specs.append(pl.BlockSpec((1, tm, SW_WIDTH), row))
        args += list(parts) + [zb]
    else:
        in_specs.append(pl.BlockSpec((1, tm, SW_WIDTH), row))
        args.append(ob)
    in_specs += [_const_spec((DN_WIDTH + SW_WIDTH, D_MODEL)), _const_spec((1, D_MODEL))]
    args += [w, fg]
    return pl.pallas_call(
        functools.partial(_out_kernel, nbranch=nbranch),
        grid=(B, L // tm),
        in_specs=in_specs,
        out_specs=pl.BlockSpec((1, tm, D_MODEL), row),
        out_shape=jax.ShapeDtypeStruct((B, L, D_MODEL), F32),
        compiler_params=pltpu.CompilerParams(
            dimension_semantics=("arbitrary", "arbitrary"), vmem_limit_bytes=VMEM_LIMIT),
        name=name,
    )(*args)


def _pad_lanes(v):
    return jnp.pad(v.astype(F32), (0, LANES - v.shape[0])).reshape(1, LANES)


def _tile(n, pref):
    return pref if n % pref == 0 else n


def _layer_prompt(x, wp, g_row, conv_w, alog_row, dtb_row, ng_row, w_out16, fg_row):
    B, L, _ = x.shape
    keep = min(WIN_MAX, L)
    dils = tuple(d for _, d in DILATIONS)
    tm = _tile(L, 512)
    qkvn, gb, za, zb, raw_last, kv_last, qb1, qb4, qb16 = _proj_prompt(
        x, g_row, wp, conv_w, alog_row, dtb_row, dils=dils, keep=keep, tm=tm)
    oa, s_new = _gdn_prompt(qkvn, gb, za, ng_row, C=DN_CHUNK, TL=tm)
    parts = []
    for qb, (win, d) in zip((qb1, qb4, qb16), DILATIONS):
        parts += _swa(qb, d=d, steps=win // d)
    y = _out(x, oa, parts, zb, None, w_out16, fg_row, tm=tm, name="out_prompt")
    new_conv = raw_last[:, HALO - (CONV_W - 1):, :]
    new_k = kv_last[:, :, :SW_WIDTH].reshape(B, keep, SW_HEADS, SW_HD)
    new_v = kv_last[:, :, SW_WIDTH:].reshape(B, keep, SW_HEADS, SW_HD)
    return y, new_conv, s_new, new_k, new_v


def _layer_sample(x, conv_state, s0, k_win, v_win, wp, g_row, conv_w, alog_row, dtb_row, ng_row,
                  w_out16, fg_row):
    B, T, _ = x.shape
    W = k_win.shape[1]
    hist = CONV_W - 1
    group = SUBLANES
    assert T + hist <= group
    rows = x.reshape(1, B * T, D_MODEL)
    raw, za, zb, ab, qkvb = _proj_sample(rows, g_row, wp, tm=_tile(B * T, 256))
    raw = raw.reshape(B, T, CONV_CH)
    front = lambda a: jnp.pad(a.reshape(B, T, -1), ((0, 0), (group - T, 0), (0, 0))).reshape(1, B * group, -1)
    ext = jnp.concatenate([jnp.zeros((B, group - T - hist, CONV_CH), F32), conv_state, raw], axis=1)
    qkvn, gb = _gdn_pre(ext.reshape(1, B * group, CONV_CH), front(ab), conv_w, alog_row, dtb_row,
                        group=group, t_new=T)
    oa, s_new = _gdn_sample(qkvn, gb, front(za), s0, ng_row, C=group, n=_tile(B * group, GDN_TILE))
    oa = oa.reshape(B, group, DN_WIDTH)[:, group - T:].reshape(1, B * T, DN_WIDTH)
    kt = jnp.transpose(k_win, (0, 2, 3, 1)).reshape(B, SW_WIDTH, W)
    vt = jnp.transpose(v_win, (0, 2, 3, 1)).reshape(B, SW_WIDTH, W)
    ob, kt_new, vt_new = _sattn(qkvb.reshape(B, T, 3 * SW_WIDTH), zb.reshape(B, T, SW_WIDTH), kt, vt)
    y = _out(rows, oa, (), None, ob.reshape(1, B * T, SW_WIDTH), w_out16, fg_row,
             tm=_tile(B * T, 256), name="out_sample")
    new_conv = jnp.concatenate([conv_state, raw], axis=1)[:, -hist:]
    new_k = jnp.transpose(kt_new.reshape(B, SW_HEADS, SW_HD, W), (0, 3, 1, 2))
    new_v = jnp.transpose(vt_new.reshape(B, SW_HEADS, SW_HD, W), (0, 3, 1, 2))
    return y.reshape(B, T, D_MODEL), new_conv, s_new, new_k, new_v


def kernel(x_prompt, x_sample, state_conv, state_delta, cache_k_win, cache_v_win, norm_g, final_norm_g,
           w_in, conv_w, a_log, dt_bias, dn_norm_g, w_out):
    assert norm_g.shape[0] == 1, "single layer"
    w = w_in[0]
    o1 = CONV_CH + DN_WIDTH
    o3 = o1 + 2 * DN_HEADS
    o4 = o3 + 3 * SW_WIDTH
    wp = jnp.concatenate(
        [w[:, :o1], w[:, o3:], w[:, o1:o3], jnp.zeros((D_MODEL, LANES - 2 * DN_HEADS), w.dtype)],
        axis=1).astype(BF16)
    assert wp.shape[1] == W_COLS and o4 - o3 == 3 * SW_WIDTH
    g_row = norm_g[0].reshape(1, D_MODEL)
    fg_row = final_norm_g.reshape(1, D_MODEL)
    alog_row = _pad_lanes(a_log[0])
    dtb_row = _pad_lanes(dt_bias[0])
    ng_row = dn_norm_g[0].reshape(1, DN_DV)
    w_out16 = w_out[0].astype(BF16)
    cw = conv_w[0]
    common = (wp, g_row, cw, alog_row, dtb_row, ng_row, w_out16, fg_row)
    yp, pc, ps, pk, pv = _layer_prompt(x_prompt, *common)
    ys, sc, ss, sk, sv = _layer_sample(x_sample, state_conv[0], state_delta[0], cache_k_win[0],
                                       cache_v_win[0], *common)
    return (yp, ys, pc[None], ps[None], pk[None], pv[None], sc[None], ss[None], sk[None], sv[None])
```

```python
import functools

import numpy as np
import jax
import jax.numpy as jnp
from jax import lax
from jax.experimental import pallas as pl
from jax.experimental.pallas import tpu as pltpu

F32 = jnp.float32
BF16 = jnp.bfloat16

D_MODEL = 1024
DN_HEADS = 4
DN_DK = 128
DN_DV = 128
QK_W = DN_HEADS * DN_DK
DN_WIDTH = DN_HEADS * DN_DV
CONV_W = 4
CONV_CH = 2 * QK_W + DN_WIDTH
DN_CHUNK = 64
SW_HEADS = 8
SW_HD = 64
SW_WIDTH = SW_HEADS * SW_HD
DILATIONS = ((128, 1), (512, 4), (2048, 16))
WIN_MAX = 2048
NORM_EPS = 1e-6

LANES = 128
SUBLANES = 8
VMEM_LIMIT = 56 * 1024 * 1024

C_RAW = 0
C_ZA = CONV_CH
C_QKVB = C_ZA + DN_WIDTH
C_ZB = C_QKVB + 3 * SW_WIDTH
C_AB = C_ZB + SW_WIDTH
W_COLS = C_AB + LANES
HALO = SUBLANES
GDN_TILE = DN_HEADS * DN_CHUNK


def _dot(a, b):
    return lax.dot_general(a, b, (((a.ndim - 1,), (0,)), ((), ())), preferred_element_type=F32)


def _dot_nt(a, b):
    return lax.dot_general(a, b, (((1,), (1,)), ((), ())), preferred_element_type=F32)


def _dot_tn(a, b):
    return lax.dot_general(a, b, (((0,), (0,)), ((), ())), preferred_element_type=F32)


def _bdot(a, b):
    return lax.dot_general(a, b, (((2,), (1,)), ((0,), (0,))), preferred_element_type=F32)


def _bdot_tn(a, b):
    return lax.dot_general(a, b, (((1,), (1,)), ((0,), (0,))), preferred_element_type=F32)


def _dot_exact_lhs(a16, x):
    hi = x.astype(BF16)
    r1 = x - hi.astype(F32)
    mid = r1.astype(BF16)
    lo = (r1 - mid.astype(F32)).astype(BF16)
    return _dot(a16, lo) + _dot(a16, mid) + _dot(a16, hi)


def _sigmoid(x):
    return 1.0 / (1.0 + jnp.exp(-x))


def _silu(x):
    return x * _sigmoid(x)


def _conv_norm_gates(ext_scr, cw_ref, alog_ref, dtb_ref, ab, qkvn_ref, gb_ref, rows, live):
    cw = cw_ref[...]
    y = cw[0:1] * ext_scr[HALO - 3:HALO - 3 + rows, :]
    y = y + cw[1:2] * ext_scr[HALO - 2:HALO - 2 + rows, :]
    y = y + cw[2:3] * ext_scr[HALO - 1:HALO - 1 + rows, :]
    y = y + cw[3:4] * ext_scr[HALO:HALO + rows, :]
    qkv = _silu(y)
    for h in range(DN_HEADS):
        sl = slice(h * DN_DK, (h + 1) * DN_DK)
        qh = qkv[:, sl]
        qkvn_ref[0, :, sl] = qh * lax.rsqrt(jnp.sum(qh * qh, -1, keepdims=True) + NORM_EPS) * (DN_DK ** -0.5)
        sl = slice(QK_W + h * DN_DK, QK_W + (h + 1) * DN_DK)
        kh = qkv[:, sl]
        qkvn_ref[0, :, sl] = kh * lax.rsqrt(jnp.sum(kh * kh, -1, keepdims=True) + NORM_EPS)
    qkvn_ref[0, :, 2 * QK_W:] = qkv[:, 2 * QK_W:]
    xg = ab + dtb_ref[...]
    softplus = jnp.maximum(xg, 0.0) + jnp.log1p(jnp.exp(-jnp.abs(xg)))
    g = -jnp.exp(alog_ref[...]) * softplus
    lane = lax.broadcasted_iota(jnp.int32, (rows, LANES), 1)
    gb = jnp.where(lane < DN_HEADS, g, _sigmoid(ab))
    if live is not None:
        gb = jnp.where(live, gb, 0.0)
    gb_ref[0] = gb


def _normed(x_ref, g_ref):
    x = x_ref[0]
    ms = jnp.mean(x * x, axis=-1, keepdims=True)
    return ((x * lax.rsqrt(ms + NORM_EPS)) * g_ref[...]).astype(BF16)


def _proj_sample_kernel(x_ref, g_ref, w_ref, raw_ref, za_ref, zb_ref, ab_ref, qb_ref):
    h = _normed(x_ref, g_ref)
    raw_ref[0] = _dot(h, w_ref[:, C_RAW:C_ZA])
    za_ref[0] = _dot(h, w_ref[:, C_ZA:C_QKVB])
    zb_ref[0] = _dot(h, w_ref[:, C_ZB:C_AB])
    ab_ref[0] = _dot(h, w_ref[:, C_AB:W_COLS])
    qb_ref[0] = _dot(h, w_ref[:, C_QKVB:C_ZB])


def _proj_prompt_kernel(x_ref, g_ref, w_ref, cw_ref, alog_ref, dtb_ref,
                        qkvn_ref, gb_ref, za_ref, zb_ref, rawlast_ref, kv_ref, *rest, dils, tm):
    qb_refs = rest[:len(dils)]
    slab_scr, ext_scr = rest[len(dils):]

    @pl.when(pl.program_id(1) == 0)
    def _():
        ext_scr[0:HALO, :] = jnp.zeros((HALO, CONV_CH), F32)

    h = _normed(x_ref, g_ref)
    ext_scr[HALO:HALO + tm, :] = _dot(h, w_ref[:, C_RAW:C_ZA])
    za_ref[0] = _dot(h, w_ref[:, C_ZA:C_QKVB])
    zb_ref[0] = _dot(h, w_ref[:, C_ZB:C_AB])
    ab = _dot(h, w_ref[:, C_AB:W_COLS])
    _conv_norm_gates(ext_scr, cw_ref, alog_ref, dtb_ref, ab, qkvn_ref, gb_ref, tm, None)
    last = ext_scr[tm:tm + HALO, :]
    rawlast_ref[0] = last
    ext_scr[0:HALO, :] = last

    qkvb = _dot(h, w_ref[:, C_QKVB:C_ZB])
    kv_ref[0] = qkvb[:, SW_WIDTH:]
    nslab = 3 * SW_WIDTH // LANES
    for j in range(nslab):
        slab_scr[j] = qkvb[:, j * LANES:(j + 1) * LANES]
    for qb_ref, d in zip(qb_refs, dils):
        if d == 1:
            qb_ref[0, 0] = qkvb.astype(BF16)
            continue
        for r in range(d):
            for j in range(nslab):
                qb_ref[0, r, :, j * LANES:(j + 1) * LANES] = (
                    slab_scr[j, pl.ds(r, tm // d, stride=d), :].astype(BF16))


def _const_spec(shape):
    zeros = (0,) * len(shape)
    return pl.BlockSpec(shape, lambda *_: zeros)


def _proj_prompt(x, g, w, cw, alog_row, dtb_row, *, dils, keep, tm):
    B, L, _ = x.shape
    row = lambda b, t: (b, t, 0)
    first = (L - keep) // tm
    out_shape = [jax.ShapeDtypeStruct((B, L, CONV_CH), F32),
                 jax.ShapeDtypeStruct((B, L, LANES), F32),
                 jax.ShapeDtypeStruct((B, L, DN_WIDTH), F32),
                 jax.ShapeDtypeStruct((B, L, SW_WIDTH), F32),
                 jax.ShapeDtypeStruct((B, HALO, CONV_CH), F32),
                 jax.ShapeDtypeStruct((B, keep, 2 * SW_WIDTH), F32)]
    out_specs = [pl.BlockSpec((1, tm, CONV_CH), row),
                 pl.BlockSpec((1, tm, LANES), row),
                 pl.BlockSpec((1, tm, DN_WIDTH), row),
                 pl.BlockSpec((1, tm, SW_WIDTH), row),
                 pl.BlockSpec((1, HALO, CONV_CH), lambda b, t: (b, 0, 0)),
                 pl.BlockSpec((1, tm, 2 * SW_WIDTH), lambda b, t: (b, jnp.maximum(t - first, 0), 0))]
    for d in dils:
        out_shape.append(jax.ShapeDtypeStruct((B, d, L // d, 3 * SW_WIDTH), BF16))
        out_specs.append(pl.BlockSpec((1, d, tm // d, 3 * SW_WIDTH), lambda b, t: (b, 0, t, 0)))
    return pl.pallas_call(
        functools.partial(_proj_prompt_kernel, dils=dils, tm=tm),
        grid=(B, L // tm),
        in_specs=[pl.BlockSpec((1, tm, D_MODEL), row),
                  _const_spec((1, D_MODEL)),
                  pl.BlockSpec((D_MODEL, W_COLS), lambda b, t: (0, 0), pipeline_mode=pl.Buffered(1)),
                  _const_spec((CONV_W, CONV_CH)),
                  _const_spec((1, LANES)),
                  _const_spec((1, LANES))],
        out_specs=out_specs,
        out_shape=out_shape,
        scratch_shapes=[pltpu.VMEM((3 * SW_WIDTH // LANES, tm, LANES), F32),
                        pltpu.VMEM((tm + HALO, CONV_CH), F32)],
        compiler_params=pltpu.CompilerParams(
            dimension_semantics=("arbitrary", "arbitrary"), vmem_limit_bytes=VMEM_LIMIT),
        name="proj_prompt",
    )(x, g, w, cw, alog_row, dtb_row)


def _proj_sample(x, g, w, *, tm):
    B, L, _ = x.shape
    row = lambda b, t: (b, t, 0)
    widths = (CONV_CH, DN_WIDTH, SW_WIDTH, LANES, 3 * SW_WIDTH)
    return pl.pallas_call(
        _proj_sample_kernel,
        grid=(B, L // tm),
        in_specs=[pl.BlockSpec((1, tm, D_MODEL), row), _const_spec((1, D_MODEL)),
                  _const_spec((D_MODEL, W_COLS))],
        out_specs=[pl.BlockSpec((1, tm, c), row) for c in widths],
        out_shape=[jax.ShapeDtypeStruct((B, L, c), F32) for c in widths],
        compiler_params=pltpu.CompilerParams(
            dimension_semantics=("arbitrary", "arbitrary"), vmem_limit_bytes=VMEM_LIMIT),
        name="proj_sample",
    )(x, g, w)


def _gdn_pre_kernel(ext_ref, ab_ref, cw_ref, alog_ref, dtb_ref, qkvn_ref, gb_ref, ext_scr, *, rows, group, t_new):
    ext_scr[0:HALO, :] = jnp.zeros((HALO, CONV_CH), F32)
    ext_scr[HALO:HALO + rows, :] = ext_ref[0]
    live = (lax.broadcasted_iota(jnp.int32, (rows, LANES), 0) & (group - 1)) >= group - t_new
    _conv_norm_gates(ext_scr, cw_ref, alog_ref, dtb_ref, ab_ref[0], qkvn_ref, gb_ref, rows, live)


def _gdn_pre(ext, ab, cw, alog_row, dtb_row, *, group, t_new):
    _, rows, _ = ext.shape
    return pl.pallas_call(
        functools.partial(_gdn_pre_kernel, rows=rows, group=group, t_new=t_new),
        grid=(1,),
        in_specs=[_const_spec((1, rows, CONV_CH)), _const_spec((1, rows, LANES)),
                  _const_spec((CONV_W, CONV_CH)), _const_spec((1, LANES)), _const_spec((1, LANES))],
        out_specs=[_const_spec((1, rows, CONV_CH)), _const_spec((1, rows, LANES))],
        out_shape=[jax.ShapeDtypeStruct((1, rows, CONV_CH), F32), jax.ShapeDtypeStruct((1, rows, LANES), F32)],
        scratch_shapes=[pltpu.VMEM((rows + HALO, CONV_CH), F32)],
        compiler_params=pltpu.CompilerParams(
            dimension_semantics=("arbitrary",), vmem_limit_bytes=VMEM_LIMIT),
        name="gdn_pre_sample",
    )(ext, ab, cw, alog_row, dtb_row)


def _block_masks(n, c):
    r = np.arange(n)[:, None]
    cc = np.arange(n)[None, :]
    x = r ^ cc
    same = x < c
    incl = same & (cc <= r)
    levels = [(x >> 1) == 0]
    k = 1
    while (2 << k) <= c:
        levels.append((x >> k) == 1)
        k += 1
    masks = np.stack([r == cc, incl] + levels).astype(np.float32)
    lmask = np.concatenate([incl, same], axis=0).astype(np.float32)
    return jnp.asarray(masks), jnp.asarray(lmask, dtype=BF16)


def _gdn_tiles(tiles, masks_ref, lmask_ref):
    n = tiles[0][0].shape[0]
    nt = len(tiles)
    nlev = masks_ref.shape[0] - 2
    eye = masks_ref[0]
    incl = masks_ref[1]
    g_b = [jnp.broadcast_to(t[3], (n, LANES)) for t in tiles]
    beta_b = [jnp.broadcast_to(t[4], (n, LANES)) for t in tiles]
    cs = _dot_exact_lhs(lmask_ref[...], jnp.concatenate(g_b, axis=1))
    gc_b = [cs[:n, i * LANES:(i + 1) * LANES] for i in range(nt)]
    gl_b = [cs[n:, i * LANES:(i + 1) * LANES] for i in range(nt)]
    kb = [tiles[i][1] * beta_b[i] for i in range(nt)]
    k16 = [t[1].astype(BF16) for t in tiles]
    kk = [_dot_nt(kb[i].astype(BF16), k16[i]) for i in range(nt)]
    qk = [_dot_nt(tiles[i][0].astype(BF16), k16[i]) for i in range(nt)]
    a, attn = [], []
    for i in range(nt):
        gc_sq = jnp.concatenate([gc_b[i]] * (n // LANES), axis=1)
        gc_row = jnp.sum(gc_sq * eye, axis=0, keepdims=True)
        dec = jnp.exp(jnp.minimum(gc_sq - gc_row, 0.0)) * incl
        a.append(kk[i] * (dec - eye))
        attn.append(qk[i] * dec)
    t = [eye - a[i] * masks_ref[2] for i in range(nt)]
    for lv in range(1, nlev):
        t16 = [x.astype(BF16) for x in t]
        p = [_dot(t16[i], (a[i] * masks_ref[2 + lv]).astype(BF16)) for i in range(nt)]
        t = [t[i] - _dot(p[i].astype(BF16), t16[i]) for i in range(nt)]
    e_gc = [jnp.exp(x) for x in gc_b]
    sol = [_dot(t[i].astype(BF16),
                jnp.concatenate([tiles[i][2] * beta_b[i], kb[i] * e_gc[i]], axis=1).astype(BF16))
           for i in range(nt)]
    return [(sol[i][:, :DN_DV], sol[i][:, DN_DV:], tiles[i][0] * e_gc[i],
             tiles[i][1] * jnp.exp(gl_b[i] - gc_b[i]), attn[i], gl_b[i]) for i in range(nt)]


def _out_norm(o, ng, za):
    return ((o * lax.rsqrt(jnp.mean(o * o, -1, keepdims=True) + NORM_EPS)) * ng) * _silu(za)


def _gdn_prompt_kernel(qkvn_ref, gb_ref, za_ref, ng_ref, masks_ref, lmask_ref, o_ref, s_out_ref, s_scr,
                       *, C, TL, G):
    H = DN_HEADS

    @pl.when(pl.program_id(1) == 0)
    def _():
        s_scr[...] = jnp.zeros_like(s_scr)

    ng = ng_ref[...]

    def group(gi, carry):
        base = pl.multiple_of(gi * (G * C), G * C)
        rows = [pl.ds(base + i * C, C) for i in range(G)]

        def stack(ref, rws, off):
            return jnp.concatenate(
                [ref[0, rws, off + h * LANES:off + (h + 1) * LANES] for h in range(H)], axis=0)

        tiles, zas = [], []
        for rws in rows:
            gbc = gb_ref[0, rws, :]
            g_col = jnp.concatenate([gbc[:, h:h + 1] for h in range(H)], axis=0)
            beta_col = jnp.concatenate([gbc[:, H + h:H + h + 1] for h in range(H)], axis=0)
            zas.append(stack(za_ref, rws, 0))
            tiles.append((stack(qkvn_ref, rws, 0), stack(qkvn_ref, rws, QK_W),
                          stack(qkvn_ref, rws, 2 * QK_W), g_col, beta_col))
        tiles = _gdn_tiles(tiles, masks_ref, lmask_ref)
        s = [s_scr[h] for h in range(H)]
        outs = []
        for (u, w, q_dec, k_dec, attn, gl_b), za in zip(tiles, zas):
            ws, qs = [], []
            for h in range(H):
                blk = slice(h * C, (h + 1) * C)
                r = _dot(jnp.concatenate([w[blk], q_dec[blk]], axis=0).astype(BF16), s[h].astype(BF16))
                ws.append(r[:C])
                qs.append(r[C:])
            v_new = u - jnp.concatenate(ws, axis=0)
            vn16 = v_new.astype(BF16)
            o = jnp.concatenate(qs, axis=0) + _dot(attn.astype(BF16), vn16)
            for h in range(H):
                blk = slice(h * C, (h + 1) * C)
                gl_s = jnp.broadcast_to(jnp.exp(gl_b[h * C:h * C + 1, :]), (DN_DK, DN_DV))
                s[h] = s[h] * gl_s + _dot_tn(k_dec[blk].astype(BF16), vn16[blk])
            outs.append(_out_norm(o, ng, za))
        for rws, out in zip(rows, outs):
            for h in range(H):
                o_ref[0, rws, h * DN_DV:(h + 1) * DN_DV] = out[h * C:(h + 1) * C].astype(o_ref.dtype)
        for h in range(H):
            s_scr[h] = s[h]
        return carry

    lax.fori_loop(0, TL // (G * C), group, 0)
    s_out_ref[0] = s_scr[...]


def _gdn_prompt(qkvn, gb, za, ng_row, *, C, TL):
    B, L, _ = qkvn.shape
    n = DN_HEADS * C
    masks, lmask = _block_masks(n, C)
    row = lambda b, t: (b, t, 0)
    return pl.pallas_call(
        functools.partial(_gdn_prompt_kernel, C=C, TL=TL, G=4 if TL % (4 * C) == 0 else 1),
        grid=(B, L // TL),
        in_specs=[pl.BlockSpec((1, TL, CONV_CH), row),
                  pl.BlockSpec((1, TL, LANES), row),
                  pl.BlockSpec((1, TL, DN_WIDTH), row),
                  _const_spec((1, DN_DV)), _const_spec(masks.shape), _const_spec(lmask.shape)],
        out_specs=[pl.BlockSpec((1, TL, DN_WIDTH), row),
                   pl.BlockSpec((1, DN_HEADS, DN_DK, DN_DV), lambda b, t: (b, 0, 0, 0))],
        out_shape=[jax.ShapeDtypeStruct((B, L, DN_WIDTH), BF16),
                   jax.ShapeDtypeStruct((B, DN_HEADS, DN_DK, DN_DV), F32)],
        scratch_shapes=[pltpu.VMEM((DN_HEADS, DN_DK, DN_DV), F32)],
        compiler_params=pltpu.CompilerParams(
            dimension_semantics=("arbitrary", "arbitrary"), vmem_limit_bytes=VMEM_LIMIT),
        name="gdn_prompt",
    )(qkvn, gb, za, ng_row, masks, lmask)


def _gdn_sample_kernel(qkvn_ref, gb_ref, za_ref, s0_ref, ng_ref, masks_ref, lmask_ref, o_ref, s_out_ref, *, C):
    n = qkvn_ref.shape[1]
    nb = n // C
    ng = ng_ref[...]
    split = lambda a: a.reshape(nb, C, a.shape[-1])
    tiles = _gdn_tiles(
        [(qkvn_ref[0, :, h * LANES:(h + 1) * LANES],
          qkvn_ref[0, :, QK_W + h * LANES:QK_W + (h + 1) * LANES],
          qkvn_ref[0, :, 2 * QK_W + h * LANES:2 * QK_W + (h + 1) * LANES],
          gb_ref[0, :, h:h + 1], gb_ref[0, :, DN_HEADS + h:DN_HEADS + h + 1]) for h in range(DN_HEADS)],
        masks_ref, lmask_ref)
    for h in range(DN_HEADS):
        sl = slice(h * LANES, (h + 1) * LANES)
        u, w, q_dec, k_dec, attn, gl_b = tiles[h]
        s0 = s0_ref[:, h]
        r = _bdot(jnp.concatenate([split(w), split(q_dec)], axis=1).astype(BF16), s0.astype(BF16))
        v_new = split(u) - r[:, :C]
        vn16 = v_new.astype(BF16)
        o = r[:, C:].reshape(n, DN_DV) + _dot(attn.astype(BF16), vn16.reshape(n, DN_DV))
        gl = split(jnp.exp(gl_b))[:, 0:1, :]
        s_out_ref[:, h] = s0 * gl + _bdot_tn(split(k_dec).astype(BF16), vn16)
        o_ref[0, :, sl] = _out_norm(o, ng, za_ref[0, :, sl]).astype(o_ref.dtype)


def _gdn_sample(qkvn, gb, za, s0, ng_row, *, C, n):
    _, rows, _ = qkvn.shape
    nb = n // C
    masks, lmask = _block_masks(n, C)
    row = lambda i: (0, i, 0)
    state = pl.BlockSpec((nb, DN_HEADS, DN_DK, DN_DV), lambda i: (i, 0, 0, 0))
    return pl.pallas_call(
        functools.partial(_gdn_sample_kernel, C=C),
        grid=(rows // n,),
        in_specs=[pl.BlockSpec((1, n, CONV_CH), row), pl.BlockSpec((1, n, LANES), row),
                  pl.BlockSpec((1, n, DN_WIDTH), row), state,
                  _const_spec((1, DN_DV)), _const_spec(masks.shape), _const_spec(lmask.shape)],
        out_specs=[pl.BlockSpec((1, n, DN_WIDTH), row), state],
        out_shape=[jax.ShapeDtypeStruct((1, rows, DN_WIDTH), BF16), jax.ShapeDtypeStruct(s0.shape, F32)],
        compiler_params=pltpu.CompilerParams(
            dimension_semantics=("arbitrary",), vmem_limit_bytes=VMEM_LIMIT),
        name="gdn_sample",
    )(qkvn, gb, za, s0, ng_row, masks, lmask)


def _swa_blocks(items):
    mq = items[0][0].shape[0]
    npair = SW_WIDTH // LANES
    nh = 2 * npair
    lo_half = lax.broadcasted_iota(jnp.int32, (mq, LANES), 1) < SW_HD
    scale = SW_HD ** -0.5
    half_masks = (jnp.where(lo_half, scale, 0.0).astype(BF16), jnp.where(lo_half, 0.0, scale).astype(BF16))
    scores = []
    for q, kk, _, _ in items:
        for p in range(npair):
            sl = slice(p * LANES, (p + 1) * LANES)
            for half in range(2):
                scores.append(_dot_nt(q[:, sl] * half_masks[half], kk[:, sl]))
    probs, inv_l, lses = [], [], []
    for i, s in enumerate(scores):
        s = jnp.where(items[i // nh][3], s, -jnp.inf)
        m = jnp.max(s, axis=-1, keepdims=True)
        pe = jnp.exp(s - m)
        l = jnp.sum(pe, axis=-1, keepdims=True)
        probs.append(pe.astype(BF16))
        inv_l.append(1.0 / l)
        lses.append(jnp.broadcast_to(m + jnp.log(l), (mq, LANES)))
    pvs = [_dot(probs[i], items[i // nh][2][:, ((i % nh) // 2) * LANES:((i % nh) // 2 + 1) * LANES])
           for i in range(len(scores))]
    out = []
    for b in range(len(items)):
        e = [b * nh + 2 * p for p in range(npair)]
        out.append([(jnp.where(lo_half, pvs[i] * inv_l[i], pvs[i + 1] * inv_l[i + 1]),
                     jnp.where(lo_half, lses[i], lses[i + 1])) for i in e])
    return out


def _swa_geometry(rt, d, blk):
    qn = rt // d
    sb = min(blk, qn)
    return qn, sb, blk // sb


def _attn_kernel(*refs, dils, rt, blk):
    nbr = len(dils)
    zb_ref, ob_ref, o_scr, lse_scr = refs[-4:]
    t = pl.program_id(1)
    npair = SW_WIDTH // LANES
    idx = 0
    for bi, d in enumerate(dils):
        qn, sb, npiece = _swa_geometry(rt, d, blk)
        q_ref, k_ref, v_ref = refs[idx:idx + 3]
        kp = refs[idx + 3:idx + 3 + npiece]
        vp = refs[idx + 3 + npiece:idx + 3 + 2 * npiece]
        idx += 3 + 2 * npiece
        nk = blk + sb
        qi = lax.broadcasted_iota(jnp.int32, (sb, nk), 0)
        kj = lax.broadcasted_iota(jnp.int32, (sb, nk), 1)
        band = (kj >= qi) & (kj <= qi + blk)
        band0 = band & (kj >= blk - t * qn)

        nsub = qn // sb

        def first(r, q_ref=q_ref, k_ref=k_ref, v_ref=v_ref, kp=kp, vp=vp, sb=sb, band0=band0):
            kk = jnp.concatenate([x[0, r] for x in kp] + [k_ref[0, r, 0:sb, :]], axis=0)
            vv = jnp.concatenate([x[0, r] for x in vp] + [v_ref[0, r, 0:sb, :]], axis=0)
            return r, 0, (q_ref[0, r, 0:sb, :], kk, vv, band0)

        def later(r, j, q_ref=q_ref, k_ref=k_ref, v_ref=v_ref, sb=sb, band=band):
            aligned = (lambda x: x) if isinstance(j, int) else (lambda x: pl.multiple_of(x, sb))
            keys = pl.ds(aligned((j - 1) * sb), 2 * sb)
            return r, j, (q_ref[0, r, pl.ds(aligned(j * sb), sb), :],
                          k_ref[0, r, keys, :], v_ref[0, r, keys, :], band)

        def run(blocks, bi=bi, d=d, sb=sb):
            for (r, j, _), res in zip(blocks, _swa_blocks([b[2] for b in blocks])):
                start = j * (sb * d) + r
                rows = pl.ds(start, sb, stride=d) if d > 1 else pl.ds(start, sb)
                for p in range(npair):
                    o_scr[bi, p, rows, :] = res[p][0]
                    lse_scr[bi, p, rows, :] = res[p][1]

        if nsub == 1:
            gs = 4
            assert d % gs == 0
            lax.fori_loop(0, d // gs, lambda g, c, run=run, first=first:
                          (run([first(gs * g + i) for i in range(gs)]), c)[1], 0)
        else:
            assert nsub % 2 == 0

            def do_class(r, run=run, first=first, later=later, nsub=nsub):
                run([first(r), later(r, 1)])
                if nsub > 2:
                    lax.fori_loop(1, nsub // 2, lambda g, c:
                                  (run([later(r, 2 * g), later(r, 2 * g + 1)]), c)[1], 0)

            if d == 1:
                do_class(0)
            else:
                lax.fori_loop(0, d, lambda r, c, f=do_class: (f(r), c)[1], 0)

    def merge(c, carry):
        rows = pl.ds(pl.multiple_of(c * blk, blk), blk)
        zb = zb_ref[0, rows, :]
        for p in range(npair):
            lses = [lse_scr[i, p, rows, :] for i in range(nbr)]
            mx = functools.reduce(jnp.maximum, lses)
            wts = [jnp.exp(l - mx) for l in lses]
            numer = sum(wts[i] * o_scr[i, p, rows, :] for i in range(nbr))
            sl = slice(p * LANES, (p + 1) * LANES)
            ob_ref[0, rows, sl] = ((numer / sum(wts)) * _silu(zb[:, sl])).astype(ob_ref.dtype)
        return carry

    lax.fori_loop(0, rt // blk, merge, 0)


def _attn_prompt(qbs, zb, *, dils, blk, rt):
    B, L, _ = zb.shape
    args, specs = [], []
    for qb, d in zip(qbs, dils):
        qn, sb, npiece = _swa_geometry(rt, d, blk)
        for c in range(3):
            specs.append(pl.BlockSpec((1, d, qn, SW_WIDTH), lambda b, t, c=c: (b, 0, t, c)))
            args.append(qb)
        for c in (1, 2):
            for pi in range(npiece):
                back, per = npiece - pi, qn // sb
                specs.append(pl.BlockSpec(
                    (1, d, sb, SW_WIDTH),
                    lambda b, t, c=c, back=back, per=per: (b, 0, jnp.maximum(t * per - back, 0), c)))
                args.append(qb)
    specs.append(pl.BlockSpec((1, rt, SW_WIDTH), lambda b, t: (b, t, 0)))
    args.append(zb)
    npair = SW_WIDTH // LANES
    return pl.pallas_call(
        functools.partial(_attn_kernel, dils=dils, rt=rt, blk=blk),
        grid=(B, L // rt),
        in_specs=specs,
        out_specs=pl.BlockSpec((1, rt, SW_WIDTH), lambda b, t: (b, t, 0)),
        out_shape=jax.ShapeDtypeStruct((B, L, SW_WIDTH), BF16),
        scratch_shapes=[pltpu.VMEM((len(dils), npair, rt, LANES), F32),
                        pltpu.VMEM((len(dils), npair, rt, LANES), F32)],
        compiler_params=pltpu.CompilerParams(
            dimension_semantics=("arbitrary", "arbitrary"), vmem_limit_bytes=VMEM_LIMIT),
        name="attn_prompt",
    )(*args)


def _sattn_kernel(qkv_ref, zb_ref, kt_ref, vt_ref, ob_ref, kto_ref, vto_ref, knew_scr, vnew_scr,
                  *, T, W, dil):
    R = T * SW_HEADS
    ncol = W // LANES

    @pl.when(pl.program_id(0) == 0)
    def _():
        knew_scr[...] = jnp.zeros_like(knew_scr)
        vnew_scr[...] = jnp.zeros_like(vnew_scr)

    qkv = qkv_ref[0]
    knew_scr[0:T, :] = qkv[:, SW_WIDTH:2 * SW_WIDTH]
    vnew_scr[0:T, :] = qkv[:, 2 * SW_WIDTH:]
    knew = knew_scr[...]
    vnew = vnew_scr[...]
    knew_t = knew.T
    vnew_t = vnew.T

    head_of_lane = lax.broadcasted_iota(jnp.int32, (SW_HEADS, SW_WIDTH), 1) >> (SW_HD.bit_length() - 1)
    head_mask = head_of_lane == lax.broadcasted_iota(jnp.int32, (SW_HEADS, SW_WIDTH), 0)
    qbd = jnp.concatenate(
        [jnp.where(head_mask, jnp.broadcast_to(qkv[t:t + 1, 0:SW_WIDTH], (SW_HEADS, SW_WIDTH)), 0.0)
         for t in range(T)], axis=0).astype(BF16)
    hm_rows = jnp.concatenate([head_mask] * T, axis=0)

    lane = lax.broadcasted_iota(jnp.int32, (SW_WIDTH, LANES), 1)
    shift = LANES - T
    scale = SW_HD ** -0.5

    s_cols = []
    k_roll = pltpu.roll(kt_ref[0, :, 0:LANES], shift, 1)
    v_roll = pltpu.roll(vt_ref[0, :, 0:LANES], shift, 1)
    for j in range(ncol):
        cols = slice(j * LANES, (j + 1) * LANES)
        s_cols.append(_dot(qbd, kt_ref[0, :, cols].astype(BF16)) * scale)
        if j + 1 < ncol:
            nxt = slice((j + 1) * LANES, (j + 2) * LANES)
            k_next = pltpu.roll(kt_ref[0, :, nxt], shift, 1)
            v_next = pltpu.roll(vt_ref[0, :, nxt], shift, 1)
        else:
            k_next = pltpu.roll(knew_t, shift, 1)
            v_next = pltpu.roll(vnew_t, shift, 1)
        kto_ref[0, :, cols] = jnp.where(lane < shift, k_roll, k_next)
        vto_ref[0, :, cols] = jnp.where(lane < shift, v_roll, v_next)
        k_roll, v_roll = k_next, v_next
    s_cols.append(_dot_nt(qbd, knew.astype(BF16)) * scale)
    s_all = jnp.concatenate(s_cols, axis=1)

    wtot = W + LANES
    j_idx = lax.broadcasted_iota(jnp.int32, (R, wtot), 1)
    t_idx = lax.broadcasted_iota(jnp.int32, (R, wtot), 0) >> (SW_HEADS.bit_length() - 1)
    dist = jnp.where(j_idx < W, W + t_idx - j_idx, t_idx - (j_idx - W))
    exists = (j_idx < W + T) & (dist >= 0)

    ms, ls, ps = [], [], []
    for win, d in dil:
        valid = exists & ((dist & (d - 1)) == 0) & (dist <= win)
        sb = jnp.where(valid, s_all, -jnp.inf)
        m = jnp.max(sb, axis=-1, keepdims=True)
        pe = jnp.exp(sb - m)
        ms.append(m)
        ls.append(jnp.sum(pe, axis=-1, keepdims=True))
        ps.append(pe.astype(BF16))
    p_all = jnp.concatenate(ps, axis=0)
    num = _dot(p_all[:, W:], vnew.astype(BF16))
    for j in range(ncol):
        cols = slice(j * LANES, (j + 1) * LANES)
        num = num + _dot_nt(p_all[:, cols], vt_ref[0, :, cols].astype(BF16))
    mx = jnp.maximum(jnp.maximum(ms[0], ms[1]), ms[2])
    wts = [jnp.exp(m - mx) for m in ms]
    numer = sum(wts[i] * num[i * R:(i + 1) * R] for i in range(len(dil)))
    denom = sum(wts[i] * ls[i] for i in range(len(dil)))
    o = jnp.where(hm_rows, numer / denom, 0.0)
    zb = zb_ref[0]
    gate = _silu(zb)
    for t in range(T):
        row = jnp.sum(o[t * SW_HEADS:(t + 1) * SW_HEADS], axis=0, keepdims=True)
        ob_ref[0, t:t + 1, :] = row * gate[t:t + 1]


def _sattn(qkvb, zb, kt, vt):
    B, T, _ = qkvb.shape
    W = kt.shape[-1]
    kern = functools.partial(_sattn_kernel, T=T, W=W, dil=DILATIONS)
    small = lambda c: pl.BlockSpec((1, T, c), lambda b: (b, 0, 0))
    big = pl.BlockSpec((1, SW_WIDTH, W), lambda b: (b, 0, 0))
    return pl.pallas_call(
        kern,
        grid=(B,),
        in_specs=[small(3 * SW_WIDTH), small(SW_WIDTH), big, big],
        out_specs=[small(SW_WIDTH), big, big],
        out_shape=[jax.ShapeDtypeStruct((B, T, SW_WIDTH), F32),
                   jax.ShapeDtypeStruct((B, SW_WIDTH, W), F32),
                   jax.ShapeDtypeStruct((B, SW_WIDTH, W), F32)],
        scratch_shapes=[pltpu.VMEM((LANES, SW_WIDTH), F32), pltpu.VMEM((LANES, SW_WIDTH), F32)],
        compiler_params=pltpu.CompilerParams(
            dimension_semantics=("arbitrary",), vmem_limit_bytes=VMEM_LIMIT),
        name="sattn",
    )(qkvb, zb, kt, vt)


def _out_kernel(x_ref, oa_ref, ob_ref, w_ref, fg_ref, y_ref):
    acc = _dot(oa_ref[0], w_ref[0:DN_WIDTH, :]) + _dot(ob_ref[0], w_ref[DN_WIDTH:, :])
    y = x_ref[0] + acc
    y_ref[0] = (y * lax.rsqrt(jnp.mean(y * y, axis=-1, keepdims=True) + NORM_EPS)) * fg_ref[...]


def _out(x, oa, ob, w, fg, *, tm, name):
    B, L, _ = x.shape
    row = lambda b, t: (b, t, 0)
    return pl.pallas_call(
        _out_kernel,
        grid=(B, L // tm),
        in_specs=[pl.BlockSpec((1, tm, D_MODEL), row), pl.BlockSpec((1, tm, DN_WIDTH), row),
                  pl.BlockSpec((1, tm, SW_WIDTH), row),
                  _const_spec((DN_WIDTH + SW_WIDTH, D_MODEL)), _const_spec((1, D_MODEL))],
        out_specs=pl.BlockSpec((1, tm, D_MODEL), row),
        out_shape=jax.ShapeDtypeStruct((B, L, D_MODEL), F32),
        compiler_params=pltpu.CompilerParams(
            dimension_semantics=("arbitrary", "arbitrary"), vmem_limit_bytes=VMEM_LIMIT),
        name=name,
    )(x, oa, ob, w, fg)


def _pad_lanes(v):
    return jnp.pad(v.astype(F32), (0, LANES - v.shape[0])).reshape(1, LANES)


def _tile(n, pref):
    return pref if n % pref == 0 else n


def _layer_prompt(x, wp, g_row, conv_w, alog_row, dtb_row, ng_row, w_out16, fg_row):
    B, L, _ = x.shape
    keep = min(WIN_MAX, L)
    dils = tuple(d for _, d in DILATIONS)
    tm = _tile(L, 512)
    qkvn, gb, za, zb, raw_last, kv_last, qb1, qb4, qb16 = _proj_prompt(
        x, g_row, wp, conv_w, alog_row, dtb_row, dils=dils, keep=keep, tm=tm)
    oa, s_new = _gdn_prompt(qkvn, gb, za, ng_row, C=DN_CHUNK, TL=tm)
    steps = {win // d for win, d in DILATIONS}
    assert len(steps) == 1, "every dilation looks back the same number of class rows"
    ob = _attn_prompt((qb1, qb4, qb16), zb, dils=dils, blk=steps.pop(), rt=_tile(L, 1024))
    y = _out(x, oa, ob, w_out16, fg_row, tm=tm, name="out_prompt")
    new_conv = raw_last[:, HALO - (CONV_W - 1):, :]
    new_k = kv_last[:, :, :SW_WIDTH].reshape(B, keep, SW_HEADS, SW_HD)
    new_v = kv_last[:, :, SW_WIDTH:].reshape(B, keep, SW_HEADS, SW_HD)
    return y, new_conv, s_new, new_k, new_v


def _layer_sample(x, conv_state, s0, k_win, v_win, wp, g_row, conv_w, alog_row, dtb_row, ng_row,
                  w_out16, fg_row):
    B, T, _ = x.shape
    W = k_win.shape[1]
    hist = CONV_W - 1
    group = SUBLANES
    assert T + hist <= group
    rows = x.reshape(1, B * T, D_MODEL)
    raw, za, zb, ab, qkvb = _proj_sample(rows, g_row, wp, tm=_tile(B * T, 256))
    raw = raw.reshape(B, T, CONV_CH)
    front = lambda a: jnp.pad(a.reshape(B, T, -1), ((0, 0), (group - T, 0), (0, 0))).reshape(1, B * group, -1)
    ext = jnp.concatenate([jnp.zeros((B, group - T - hist, CONV_CH), F32), conv_state, raw], axis=1)
    qkvn, gb = _gdn_pre(ext.reshape(1, B * group, CONV_CH), front(ab), conv_w, alog_row, dtb_row,
                        group=group, t_new=T)
    oa, s_new = _gdn_sample(qkvn, gb, front(za), s0, ng_row, C=group, n=_tile(B * group, GDN_TILE))
    oa = oa.reshape(B, group, DN_WIDTH)[:, group - T:].reshape(1, B * T, DN_WIDTH)
    kt = jnp.transpose(k_win, (0, 2, 3, 1)).reshape(B, SW_WIDTH, W)
    vt = jnp.transpose(v_win, (0, 2, 3, 1)).reshape(B, SW_WIDTH, W)
    ob, kt_new, vt_new = _sattn(qkvb.reshape(B, T, 3 * SW_WIDTH), zb.reshape(B, T, SW_WIDTH), kt, vt)
    y = _out(rows, oa, ob.reshape(1, B * T, SW_WIDTH).astype(BF16), w_out16, fg_row,
             tm=_tile(B * T, 256), name="out_sample")
    new_conv = jnp.concatenate([conv_state, raw], axis=1)[:, -hist:]
    new_k = jnp.transpose(kt_new.reshape(B, SW_HEADS, SW_HD, W), (0, 3, 1, 2))
    new_v = jnp.transpose(vt_new.reshape(B, SW_HEADS, SW_HD, W), (0, 3, 1, 2))
    return y.reshape(B, T, D_MODEL), new_conv, s_new, new_k, new_v


def kernel(x_prompt, x_sample, state_conv, state_delta, cache_k_win, cache_v_win, norm_g, final_norm_g,
           w_in, conv_w, a_log, dt_bias, dn_norm_g, w_out):
    assert norm_g.shape[0] == 1, "single layer"
    w = w_in[0]
    o1 = CONV_CH + DN_WIDTH
    o3 = o1 + 2 * DN_HEADS
    o4 = o3 + 3 * SW_WIDTH
    wp = jnp.concatenate(
        [w[:, :o1], w[:, o3:], w[:, o1:o3], jnp.zeros((D_MODEL, LANES - 2 * DN_HEADS), w.dtype)],
        axis=1).astype(BF16)
    assert wp.shape[1] == W_COLS and o4 - o3 == 3 * SW_WIDTH
    g_row = norm_g[0].reshape(1, D_MODEL)
    fg_row = final_norm_g.reshape(1, D_MODEL)
    alog_row = _pad_lanes(a_log[0])
    dtb_row = _pad_lanes(dt_bias[0])
    ng_row = dn_norm_g[0].reshape(1, DN_DV)
    w_out16 = w_out[0].astype(BF16)
    cw = conv_w[0]
    common = (wp, g_row, cw, alog_row, dtb_row, ng_row, w_out16, fg_row)
    yp, pc, ps, pk, pv = _layer_prompt(x_prompt, *common)
    ys, sc, ss, sk, sv = _layer_sample(x_sample, state_conv[0], state_delta[0], cache_k_win[0],
                                       cache_v_win[0], *common)
    return (yp, ys, pc[None], ps[None], pk[None], pv[None], sc[None], ss[None], sk[None], sv[None])
```

```python
import functools

import numpy as np
import jax
import jax.numpy as jnp
from jax import lax
from jax.experimental import pallas as pl
from jax.experimental.pallas import tpu as pltpu

F32 = jnp.float32
BF16 = jnp.bfloat16

D_MODEL = 1024
DN_HEADS = 4
DN_DK = 128
DN_DV = 128
QK_W = DN_HEADS * DN_DK
DN_WIDTH = DN_HEADS * DN_DV
CONV_W = 4
CONV_CH = 2 * QK_W + DN_WIDTH
DN_CHUNK = 64
SW_HEADS = 8
SW_HD = 64
SW_WIDTH = SW_HEADS * SW_HD
DILATIONS = ((128, 1), (512, 4), (2048, 16))
WIN_MAX = 2048
NORM_EPS = 1e-6

LANES = 128
SUBLANES = 8
VMEM_LIMIT = 56 * 1024 * 1024

C_RAW = 0
C_ZA = CONV_CH
C_QKVB = C_ZA + DN_WIDTH
C_ZB = C_QKVB + 3 * SW_WIDTH
C_AB = C_ZB + SW_WIDTH
W_COLS = C_AB + LANES
HALO = SUBLANES
GDN_TILE = DN_HEADS * DN_CHUNK


def _dot(a, b):
    return lax.dot_general(a, b, (((a.ndim - 1,), (0,)), ((), ())), preferred_element_type=F32)


def _dot_nt(a, b):
    return lax.dot_general(a, b, (((1,), (1,)), ((), ())), preferred_element_type=F32)


def _dot_tn(a, b):
    return lax.dot_general(a, b, (((0,), (0,)), ((), ())), preferred_element_type=F32)


def _bdot(a, b):
    return lax.dot_general(a, b, (((2,), (1,)), ((0,), (0,))), preferred_element_type=F32)


def _bdot_tn(a, b):
    return lax.dot_general(a, b, (((1,), (1,)), ((0,), (0,))), preferred_element_type=F32)


def _dot_exact_lhs(a16, x):
    hi = x.astype(BF16)
    r1 = x - hi.astype(F32)
    mid = r1.astype(BF16)
    lo = (r1 - mid.astype(F32)).astype(BF16)
    return _dot(a16, lo) + _dot(a16, mid) + _dot(a16, hi)


def _sigmoid(x):
    return 1.0 / (1.0 + jnp.exp(-x))


def _silu(x):
    return x * _sigmoid(x)


def _conv_norm_gates(ext_scr, cw_ref, alog_ref, dtb_ref, qkvn_ref, gb_ref, rows, live_from):
    ch = LANES if rows % LANES == 0 else rows
    cw = cw_ref[...]
    a_neg = -jnp.exp(alog_ref[...])
    dtb = dtb_ref[...]
    lane = lax.broadcasted_iota(jnp.int32, (ch, LANES), 1)

    def chunk(c, carry):
        r0 = c * ch
        out = pl.ds(r0, ch)
        for s in range(CONV_CH // LANES):
            sl = slice(s * LANES, (s + 1) * LANES)
            win = ext_scr[pl.ds(r0, ch + HALO), sl]
            y = cw[0:1, sl] * win[HALO - 3:HALO - 3 + ch]
            y = y + cw[1:2, sl] * win[HALO - 2:HALO - 2 + ch]
            y = y + cw[2:3, sl] * win[HALO - 1:HALO - 1 + ch]
            y = y + cw[3:4, sl] * win[HALO:HALO + ch]
            v = _silu(y)
            if s < DN_HEADS:
                v = v * lax.rsqrt(jnp.sum(v * v, -1, keepdims=True) + NORM_EPS) * (DN_DK ** -0.5)
            elif s < 2 * DN_HEADS:
                v = v * lax.rsqrt(jnp.sum(v * v, -1, keepdims=True) + NORM_EPS)
            qkvn_ref[0, out, sl] = v
        ab = gb_ref[0, out, :]
        xg = ab + dtb
        softplus = jnp.maximum(xg, 0.0) + jnp.log1p(jnp.exp(-jnp.abs(xg)))
        gb = jnp.where(lane < DN_HEADS, a_neg * softplus, _sigmoid(ab))
        if live_from is not None:
            group, first = live_from
            rowi = lax.broadcasted_iota(jnp.int32, (ch, LANES), 0) + r0
            gb = jnp.where((rowi & (group - 1)) >= first, gb, 0.0)
        gb_ref[0, out, :] = gb
        return carry

    for c in range(rows // ch):
        chunk(c, 0)


def _normed(x_ref, g_ref, h_scr):
    rows = x_ref.shape[1]
    ch = LANES if rows % LANES == 0 else rows
    g = g_ref[...]

    for c in range(rows // ch):
        rws = pl.ds(c * ch, ch)
        x = x_ref[0, rws, :]
        ms = jnp.mean(x * x, axis=-1, keepdims=True)
        h_scr[rws, :] = ((x * lax.rsqrt(ms + NORM_EPS)) * g).astype(BF16)
    return h_scr[...]


def _proj_sample_kernel(x_ref, g_ref, w_ref, raw_ref, za_ref, zb_ref, ab_ref, qb_ref, h_scr):
    h = _normed(x_ref, g_ref, h_scr)
    raw_ref[0] = _dot(h, w_ref[:, C_RAW:C_ZA])
    za_ref[0] = _dot(h, w_ref[:, C_ZA:C_QKVB])
    zb_ref[0] = _dot(h, w_ref[:, C_ZB:C_AB])
    ab_ref[0] = _dot(h, w_ref[:, C_AB:W_COLS])
    qb_ref[0] = _dot(h, w_ref[:, C_QKVB:C_ZB])


def _proj_prompt_kernel(x_ref, g_ref, w_ref, cw_ref, alog_ref, dtb_ref,
                        qkvn_ref, gb_ref, za_ref, zb_ref, rawlast_ref, kv_ref, *rest, dils, tm):
    nd = len(dils)
    qb_refs = rest[:nd]
    cls_scrs = list(rest[nd:2 * nd - 1]) + [None]
    ext_scr, h_scr = rest[2 * nd - 1:]

    @pl.when(pl.program_id(1) == 0)
    def _():
        ext_scr[0:HALO, :] = jnp.zeros((HALO, CONV_CH), F32)

    _normed(x_ref, g_ref, h_scr)
    qkvb = _dot(h_scr[...], w_ref[:, C_QKVB:C_ZB])
    ext_scr[HALO:HALO + tm, :] = _dot(h_scr[...], w_ref[:, C_RAW:C_ZA])
    za_ref[0] = _dot(h_scr[...], w_ref[:, C_ZA:C_QKVB])
    zb_ref[0] = _dot(h_scr[...], w_ref[:, C_ZB:C_AB])
    gb_ref[0] = _dot(h_scr[...], w_ref[:, C_AB:W_COLS])

    kv_ref[0] = qkvb[:, SW_WIDTH:]
    nslab = 3 * SW_WIDTH // LANES
    assert dils[0] == 1
    qb_refs[0][0, 0] = qkvb.astype(BF16)
    for j in range(nslab):
        cls_scrs[0][j] = qkvb[:, j * LANES:(j + 1) * LANES]
    for i in range(1, nd):
        d, prev = dils[i], dils[i - 1]
        ratio, n_d, n_prev = d // prev, tm // d, tm // prev
        assert d % prev == 0
        for r in range(d):
            for j in range(nslab):
                v = cls_scrs[i - 1][j, pl.ds((r % prev) * n_prev + r // prev, n_d, stride=ratio), :]
                qb_refs[i][0, r, :, j * LANES:(j + 1) * LANES] = v.astype(BF16)
                if cls_scrs[i] is not None:
                    cls_scrs[i][j, r * n_d:(r + 1) * n_d, :] = v

    _conv_norm_gates(ext_scr, cw_ref, alog_ref, dtb_ref, qkvn_ref, gb_ref, tm, None)
    last = ext_scr[tm:tm + HALO, :]
    rawlast_ref[0] = last
    ext_scr[0:HALO, :] = last


def _const_spec(shape):
    zeros = (0,) * len(shape)
    return pl.BlockSpec(shape, lambda *_: zeros)


def _proj_prompt(x, g, w, cw, alog_row, dtb_row, *, dils, keep, tm):
    B, L, _ = x.shape
    row = lambda b, t: (b, t, 0)
    first = (L - keep) // tm
    out_shape = [jax.ShapeDtypeStruct((B, L, CONV_CH), F32),
                 jax.ShapeDtypeStruct((B, L, LANES), F32),
                 jax.ShapeDtypeStruct((B, L, DN_WIDTH), F32),
                 jax.ShapeDtypeStruct((B, L, SW_WIDTH), F32),
                 jax.ShapeDtypeStruct((B, HALO, CONV_CH), F32),
                 jax.ShapeDtypeStruct((B, keep, 2 * SW_WIDTH), F32)]
    out_specs = [pl.BlockSpec((1, tm, CONV_CH), row),
                 pl.BlockSpec((1, tm, LANES), row),
                 pl.BlockSpec((1, tm, DN_WIDTH), row),
                 pl.BlockSpec((1, tm, SW_WIDTH), row),
                 pl.BlockSpec((1, HALO, CONV_CH), lambda b, t: (b, 0, 0)),
                 pl.BlockSpec((1, tm, 2 * SW_WIDTH), lambda b, t: (b, jnp.maximum(t - first, 0), 0))]
    for d in dils:
        out_shape.append(jax.ShapeDtypeStruct((B, d, L // d, 3 * SW_WIDTH), BF16))
        out_specs.append(pl.BlockSpec((1, d, tm // d, 3 * SW_WIDTH), lambda b, t: (b, 0, t, 0)))
    return pl.pallas_call(
        functools.partial(_proj_prompt_kernel, dils=dils, tm=tm),
        grid=(B, L // tm),
        in_specs=[pl.BlockSpec((1, tm, D_MODEL), row),
                  _const_spec((1, D_MODEL)),
                  pl.BlockSpec((D_MODEL, W_COLS), lambda b, t: (0, 0), pipeline_mode=pl.Buffered(1)),
                  _const_spec((CONV_W, CONV_CH)),
                  _const_spec((1, LANES)),
                  _const_spec((1, LANES))],
        out_specs=out_specs,
        out_shape=out_shape,
        scratch_shapes=[pltpu.VMEM((3 * SW_WIDTH // LANES, tm, LANES), F32)] * (len(dils) - 1)
        + [pltpu.VMEM((tm + HALO, CONV_CH), F32), pltpu.VMEM((tm, D_MODEL), BF16)],
        compiler_params=pltpu.CompilerParams(
            dimension_semantics=("arbitrary", "arbitrary"), vmem_limit_bytes=VMEM_LIMIT),
        name="proj_prompt",
    )(x, g, w, cw, alog_row, dtb_row)


def _proj_sample(x, g, w, *, tm):
    B, L, _ = x.shape
    row = lambda b, t: (b, t, 0)
    widths = (CONV_CH, DN_WIDTH, SW_WIDTH, LANES, 3 * SW_WIDTH)
    return pl.pallas_call(
        _proj_sample_kernel,
        grid=(B, L // tm),
        in_specs=[pl.BlockSpec((1, tm, D_MODEL), row), _const_spec((1, D_MODEL)),
                  _const_spec((D_MODEL, W_COLS))],
        out_specs=[pl.BlockSpec((1, tm, c), row) for c in widths],
        out_shape=[jax.ShapeDtypeStruct((B, L, c), F32) for c in widths],
        scratch_shapes=[pltpu.VMEM((tm, D_MODEL), BF16)],
        compiler_params=pltpu.CompilerParams(
            dimension_semantics=("arbitrary", "arbitrary"), vmem_limit_bytes=VMEM_LIMIT),
        name="proj_sample",
    )(x, g, w)


def _gdn_pre_kernel(ext_ref, ab_ref, cw_ref, alog_ref, dtb_ref, qkvn_ref, gb_ref, ext_scr, *, rows, group, t_new):
    ext_scr[0:HALO, :] = jnp.zeros((HALO, CONV_CH), F32)
    ext_scr[HALO:HALO + rows, :] = ext_ref[0]
    gb_ref[0] = ab_ref[0]
    _conv_norm_gates(ext_scr, cw_ref, alog_ref, dtb_ref, qkvn_ref, gb_ref, rows, (group, group - t_new))


def _gdn_pre(ext, ab, cw, alog_row, dtb_row, *, group, t_new):
    _, rows, _ = ext.shape
    return pl.pallas_call(
        functools.partial(_gdn_pre_kernel, rows=rows, group=group, t_new=t_new),
        grid=(1,),
        in_specs=[_const_spec((1, rows, CONV_CH)), _const_spec((1, rows, LANES)),
                  _const_spec((CONV_W, CONV_CH)), _const_spec((1, LANES)), _const_spec((1, LANES))],
        out_specs=[_const_spec((1, rows, CONV_CH)), _const_spec((1, rows, LANES))],
        out_shape=[jax.ShapeDtypeStruct((1, rows, CONV_CH), F32), jax.ShapeDtypeStruct((1, rows, LANES), F32)],
        scratch_shapes=[pltpu.VMEM((rows + HALO, CONV_CH), F32)],
        compiler_params=pltpu.CompilerParams(
            dimension_semantics=("arbitrary",), vmem_limit_bytes=VMEM_LIMIT),
        name="gdn_pre_sample",
    )(ext, ab, cw, alog_row, dtb_row)


def _block_masks(n, c):
    r = np.arange(n)[:, None]
    cc = np.arange(n)[None, :]
    x = r ^ cc
    same = x < c
    incl = same & (cc <= r)
    levels = [(x >> 1) == 0]
    k = 1
    while (2 << k) <= c:
        levels.append((x >> k) == 1)
        k += 1
    masks = np.stack([r == cc, incl] + levels).astype(np.float32)
    lmask = np.concatenate([incl, same], axis=0).astype(np.float32)
    return jnp.asarray(masks), jnp.asarray(lmask, dtype=BF16)


def _gdn_tiles(tiles, masks_ref, lmask_ref):
    n = tiles[0][0].shape[0]
    nt = len(tiles)
    nlev = masks_ref.shape[0] - 2
    eye = masks_ref[0]
    incl = masks_ref[1]
    g_b = [jnp.broadcast_to(t[3], (n, LANES)) for t in tiles]
    beta_b = [jnp.broadcast_to(t[4], (n, LANES)) for t in tiles]
    cs = _dot_exact_lhs(lmask_ref[...], jnp.concatenate(g_b, axis=1))
    gc_b = [cs[:n, i * LANES:(i + 1) * LANES] for i in range(nt)]
    gl_b = [cs[n:, i * LANES:(i + 1) * LANES] for i in range(nt)]
    kb = [tiles[i][1] * beta_b[i] for i in range(nt)]
    k16 = [t[1].astype(BF16) for t in tiles]
    kk = [_dot_nt(kb[i].astype(BF16), k16[i]) for i in range(nt)]
    qk = [_dot_nt(tiles[i][0].astype(BF16), k16[i]) for i in range(nt)]
    a, attn = [], []
    for i in range(nt):
        gc_sq = jnp.concatenate([gc_b[i]] * (n // LANES), axis=1)
        gc_row = jnp.sum(gc_sq * eye, axis=0, keepdims=True)
        dec = jnp.exp(jnp.minimum(gc_sq - gc_row, 0.0)) * incl
        a.append(kk[i] * (dec - eye))
        attn.append(qk[i] * dec)
    t = [eye - a[i] * masks_ref[2] for i in range(nt)]
    for lv in range(1, nlev):
        t16 = [x.astype(BF16) for x in t]
        p = [_dot(t16[i], (a[i] * masks_ref[2 + lv]).astype(BF16)) for i in range(nt)]
        t = [t[i] - _dot(p[i].astype(BF16), t16[i]) for i in range(nt)]
    e_gc = [jnp.exp(x) for x in gc_b]
    sol = [_dot(t[i].astype(BF16),
                jnp.concatenate([tiles[i][2] * beta_b[i], kb[i] * e_gc[i]], axis=1).astype(BF16))
           for i in range(nt)]
    return [(sol[i][:, :DN_DV], sol[i][:, DN_DV:], tiles[i][0] * e_gc[i],
             tiles[i][1] * jnp.exp(gl_b[i] - gc_b[i]), attn[i], gl_b[i]) for i in range(nt)]


def _out_norm(o, ng, za):
    return ((o * lax.rsqrt(jnp.mean(o * o, -1, keepdims=True) + NORM_EPS)) * ng) * _silu(za)


def _gdn_prompt_kernel(qkvn_ref, gb_ref, za_ref, ng_ref, masks_ref, lmask_ref, kt_ref, vt_ref, new_ref,
                       o_ref, s_out_ref, kto_ref, vto_ref, s_scr, knew_scr, vnew_scr, *, C, TL, G):
    H = DN_HEADS

    @pl.when(pl.program_id(1) == 0)
    def _():
        s_scr[...] = jnp.zeros_like(s_scr)

    @pl.when((pl.program_id(0) == 0) & (pl.program_id(1) == 0))
    def _():
        knew_scr[...] = jnp.zeros_like(knew_scr)
        vnew_scr[...] = jnp.zeros_like(vnew_scr)

    for u in range(kt_ref.shape[0]):
        _shift_window(kt_ref, u, new_ref[u, :, SW_WIDTH:2 * SW_WIDTH], knew_scr, kto_ref)
        _shift_window(vt_ref, u, new_ref[u, :, 2 * SW_WIDTH:], vnew_scr, vto_ref)

    ng = ng_ref[...]

    def group(gi, carry):
        base = gi * (G * C) if isinstance(gi, int) else pl.multiple_of(gi * (G * C), G * C)
        rows = [pl.ds(base + i * C, C) for i in range(G)]

        def stack(ref, rws, off):
            return jnp.concatenate(
                [ref[0, rws, off + h * LANES:off + (h + 1) * LANES] for h in range(H)], axis=0)

        tiles, zas = [], []
        for rws in rows:
            gbc = gb_ref[0, rws, :]
            g_col = jnp.concatenate([gbc[:, h:h + 1] for h in range(H)], axis=0)
            beta_col = jnp.concatenate([gbc[:, H + h:H + h + 1] for h in range(H)], axis=0)
            zas.append(stack(za_ref, rws, 0))
            tiles.append((stack(qkvn_ref, rws, 0), stack(qkvn_ref, rws, QK_W),
                          stack(qkvn_ref, rws, 2 * QK_W), g_col, beta_col))
        tiles = _gdn_tiles(tiles, masks_ref, lmask_ref)
        s = [s_scr[h] for h in range(H)]
        outs = []
        for (u, w, q_dec, k_dec, attn, gl_b), za in zip(tiles, zas):
            ws, qs = [], []
            for h in range(H):
                blk = slice(h * C, (h + 1) * C)
                r = _dot(jnp.concatenate([w[blk], q_dec[blk]], axis=0).astype(BF16), s[h].astype(BF16))
                ws.append(r[:C])
                qs.append(r[C:])
            v_new = u - jnp.concatenate(ws, axis=0)
            vn16 = v_new.astype(BF16)
            o = jnp.concatenate(qs, axis=0) + _dot(attn.astype(BF16), vn16)
            for h in range(H):
                blk = slice(h * C, (h + 1) * C)
                gl_s = jnp.broadcast_to(jnp.exp(gl_b[h * C:h * C + 1, :]), (DN_DK, DN_DV))
                s[h] = s[h] * gl_s + _dot_tn(k_dec[blk].astype(BF16), vn16[blk])
            outs.append(_out_norm(o, ng, za))
        for rws, out in zip(rows, outs):
            for h in range(H):
                o_ref[0, rws, h * DN_DV:(h + 1) * DN_DV] = out[h * C:(h + 1) * C].astype(o_ref.dtype)
        for h in range(H):
            s_scr[h] = s[h]
        return carry

    if TL == G * C:
        group(0, 0)
    else:
        lax.fori_loop(0, TL // (G * C), group, 0)
    s_out_ref[0] = s_scr[...]


def _gdn_prompt(qkvn, gb, za, ng_row, kt, vt, new_rows, *, C, TL):
    B, L, _ = qkvn.shape
    n = DN_HEADS * C
    nt = L // TL
    Bs, T, _ = new_rows.shape
    W = kt.shape[-1]
    assert Bs % (B * nt) == 0
    per = Bs // (B * nt)
    masks, lmask = _block_masks(n, C)
    row = lambda b, t: (b, t, 0)
    share = lambda b, t: (b * nt + t, 0, 0)
    window = pl.BlockSpec((per, SW_WIDTH, W), share)
    return pl.pallas_call(
        functools.partial(_gdn_prompt_kernel, C=C, TL=TL, G=4 if TL % (4 * C) == 0 else 1),
        grid=(B, nt),
        in_specs=[pl.BlockSpec((1, TL, CONV_CH), row),
                  pl.BlockSpec((1, TL, LANES), row),
                  pl.BlockSpec((1, TL, DN_WIDTH), row),
                  _const_spec((1, DN_DV)), _const_spec(masks.shape), _const_spec(lmask.shape),
                  window, window, pl.BlockSpec((per, T, 3 * SW_WIDTH), share)],
        out_specs=[pl.BlockSpec((1, TL, DN_WIDTH), row),
                   pl.BlockSpec((1, DN_HEADS, DN_DK, DN_DV), lambda b, t: (b, 0, 0, 0)),
                   window, window],
        out_shape=[jax.ShapeDtypeStruct((B, L, DN_WIDTH), BF16),
                   jax.ShapeDtypeStruct((B, DN_HEADS, DN_DK, DN_DV), F32),
                   jax.ShapeDtypeStruct(kt.shape, F32), jax.ShapeDtypeStruct(vt.shape, F32)],
        scratch_shapes=[pltpu.VMEM((DN_HEADS, DN_DK, DN_DV), F32),
                        pltpu.VMEM((LANES, SW_WIDTH), F32), pltpu.VMEM((LANES, SW_WIDTH), F32)],
        compiler_params=pltpu.CompilerParams(
            dimension_semantics=("arbitrary", "arbitrary"), vmem_limit_bytes=VMEM_LIMIT),
        name="gdn_prompt",
    )(qkvn, gb, za, ng_row, masks, lmask, kt, vt, new_rows)


def _gdn_sample_kernel(qkvn_ref, gb_ref, za_ref, s0_ref, ng_ref, masks_ref, lmask_ref, o_ref, s_out_ref, *, C):
    n = qkvn_ref.shape[1]
    nb = n // C
    ng = ng_ref[...]
    split = lambda a: a.reshape(nb, C, a.shape[-1])
    tiles = _gdn_tiles(
        [(qkvn_ref[0, :, h * LANES:(h + 1) * LANES],
          qkvn_ref[0, :, QK_W + h * LANES:QK_W + (h + 1) * LANES],
          qkvn_ref[0, :, 2 * QK_W + h * LANES:2 * QK_W + (h + 1) * LANES],
          gb_ref[0, :, h:h + 1], gb_ref[0, :, DN_HEADS + h:DN_HEADS + h + 1]) for h in range(DN_HEADS)],
        masks_ref, lmask_ref)
    for h in range(DN_HEADS):
        sl = slice(h * LANES, (h + 1) * LANES)
        u, w, q_dec, k_dec, attn, gl_b = tiles[h]
        s0 = s0_ref[:, h]
        r = _bdot(jnp.concatenate([split(w), split(q_dec)], axis=1).astype(BF16), s0.astype(BF16))
        v_new = split(u) - r[:, :C]
        vn16 = v_new.astype(BF16)
        o = r[:, C:].reshape(n, DN_DV) + _dot(attn.astype(BF16), vn16.reshape(n, DN_DV))
        gl = split(jnp.exp(gl_b))[:, 0:1, :]
        s_out_ref[:, h] = s0 * gl + _bdot_tn(split(k_dec).astype(BF16), vn16)
        o_ref[0, :, sl] = _out_norm(o, ng, za_ref[0, :, sl]).astype(o_ref.dtype)


def _gdn_sample(qkvn, gb, za, s0, ng_row, *, C, n):
    _, rows, _ = qkvn.shape
    nb = n // C
    masks, lmask = _block_masks(n, C)
    row = lambda i: (0, i, 0)
    state = pl.BlockSpec((nb, DN_HEADS, DN_DK, DN_DV), lambda i: (i, 0, 0, 0))
    return pl.pallas_call(
        functools.partial(_gdn_sample_kernel, C=C),
        grid=(rows // n,),
        in_specs=[pl.BlockSpec((1, n, CONV_CH), row), pl.BlockSpec((1, n, LANES), row),
                  pl.BlockSpec((1, n, DN_WIDTH), row), state,
                  _const_spec((1, DN_DV)), _const_spec(masks.shape), _const_spec(lmask.shape)],
        out_specs=[pl.BlockSpec((1, n, DN_WIDTH), row), state],
        out_shape=[jax.ShapeDtypeStruct((1, rows, DN_WIDTH), BF16), jax.ShapeDtypeStruct(s0.shape, F32)],
        compiler_params=pltpu.CompilerParams(
            dimension_semantics=("arbitrary",), vmem_limit_bytes=VMEM_LIMIT),
        name="gdn_sample",
    )(qkvn, gb, za, s0, ng_row, masks, lmask)


def _swa_blocks(items):
    mq = items[0][0].shape[0]
    npair = SW_WIDTH // LANES
    nh = 2 * npair
    lo_half = lax.broadcasted_iota(jnp.int32, (mq, LANES), 1) < SW_HD
    scale = SW_HD ** -0.5
    half_masks = (jnp.where(lo_half, scale, 0.0).astype(BF16), jnp.where(lo_half, 0.0, scale).astype(BF16))
    scores = []
    for q, kk, _, _ in items:
        for p in range(npair):
            sl = slice(p * LANES, (p + 1) * LANES)
            for half in range(2):
                scores.append(_dot_nt(q[:, sl] * half_masks[half], kk[:, sl]))
    probs, inv_l, lses = [], [], []
    for i, s in enumerate(scores):
        s = jnp.where(items[i // nh][3], s, -jnp.inf)
        m = jnp.max(s, axis=-1, keepdims=True)
        pe = jnp.exp(s - m)
        l = jnp.sum(pe, axis=-1, keepdims=True)
        probs.append(pe.astype(BF16))
        inv_l.append(1.0 / l)
        lses.append(jnp.broadcast_to(m + jnp.log(l), (mq, LANES)))
    pvs = [_dot(probs[i], items[i // nh][2][:, ((i % nh) // 2) * LANES:((i % nh) // 2 + 1) * LANES])
           for i in range(len(scores))]
    out = []
    for b in range(len(items)):
        e = [b * nh + 2 * p for p in range(npair)]
        out.append([(jnp.where(lo_half, pvs[i] * inv_l[i], pvs[i + 1] * inv_l[i + 1]),
                     jnp.where(lo_half, lses[i], lses[i + 1])) for i in e])
    return out


def _swa_geometry(rt, d, blk):
    qn = rt // d
    sb = min(blk, qn)
    return qn, sb, blk // sb


def _attn_kernel(*refs, dils, rt, blk):
    nbr = len(dils)
    zb_ref, ob_ref, o_scr, lse_scr = refs[-4:]
    t = pl.program_id(1)
    npair = SW_WIDTH // LANES
    idx = 0
    for bi, d in enumerate(dils):
        qn, sb, npiece = _swa_geometry(rt, d, blk)
        q_ref, k_ref, v_ref = refs[idx:idx + 3]
        kp = refs[idx + 3:idx + 3 + npiece]
        vp = refs[idx + 3 + npiece:idx + 3 + 2 * npiece]
        idx += 3 + 2 * npiece
        nk = blk + sb
        qi = lax.broadcasted_iota(jnp.int32, (sb, nk), 0)
        kj = lax.broadcasted_iota(jnp.int32, (sb, nk), 1)
        band = (kj >= qi) & (kj <= qi + blk)
        band0 = band & (kj >= blk - t * qn)

        nsub = qn // sb

        def first(r, q_ref=q_ref, k_ref=k_ref, v_ref=v_ref, kp=kp, vp=vp, sb=sb, band0=band0):
            kk = jnp.concatenate([x[0, r] for x in kp] + [k_ref[0, r, 0:sb, :]], axis=0)
            vv = jnp.concatenate([x[0, r] for x in vp] + [v_ref[0, r, 0:sb, :]], axis=0)
            return r, 0, (q_ref[0, r, 0:sb, :], kk, vv, band0)

        def later(r, j, q_ref=q_ref, k_ref=k_ref, v_ref=v_ref, sb=sb, band=band):
            aligned = (lambda x: x) if isinstance(j, int) else (lambda x: pl.multiple_of(x, sb))
            keys = pl.ds(aligned((j - 1) * sb), 2 * sb)
            return r, j, (q_ref[0, r, pl.ds(aligned(j * sb), sb), :],
                          k_ref[0, r, keys, :], v_ref[0, r, keys, :], band)

        def run(blocks, bi=bi, d=d, sb=sb):
            for (r, j, _), res in zip(blocks, _swa_blocks([b[2] for b in blocks])):
                start = j * (sb * d) + r
                rows = pl.ds(start, sb, stride=d) if d > 1 else pl.ds(start, sb)
                for p in range(npair):
                    o_scr[bi, p, rows, :] = res[p][0]
                    lse_scr[bi, p, rows, :] = res[p][1]

        if nsub == 1:
            gs = 4
            assert d % gs == 0
            lax.fori_loop(0, d // gs, lambda g, c, run=run, first=first:
                          (run([first(gs * g + i) for i in range(gs)]), c)[1], 0)
        else:
            assert nsub % 2 == 0

            def do_class(r, run=run, first=first, later=later, nsub=nsub):
                run([first(r), later(r, 1)])
                if nsub > 2:
                    lax.fori_loop(1, nsub // 2, lambda g, c:
                                  (run([later(r, 2 * g), later(r, 2 * g + 1)]), c)[1], 0)

            if d == 1:
                do_class(0)
            else:
                lax.fori_loop(0, d, lambda r, c, f=do_class: (f(r), c)[1], 0)

    def merge(c, carry):
        rows = pl.ds(pl.multiple_of(c * blk, blk), blk)
        zb = zb_ref[0, rows, :]
        for p in range(npair):
            lses = [lse_scr[i, p, rows, :] for i in range(nbr)]
            mx = functools.reduce(jnp.maximum, lses)
            wts = [jnp.exp(l - mx) for l in lses]
            numer = sum(wts[i] * o_scr[i, p, rows, :] for i in range(nbr))
            sl = slice(p * LANES, (p + 1) * LANES)
            ob_ref[0, rows, sl] = ((numer / sum(wts)) * _silu(zb[:, sl])).astype(ob_ref.dtype)
        return carry

    lax.fori_loop(0, rt // blk, merge, 0)


def _attn_prompt(qbs, zb, *, dils, blk, rt):
    B, L, _ = zb.shape
    args, specs = [], []
    for qb, d in zip(qbs, dils):
        qn, sb, npiece = _swa_geometry(rt, d, blk)
        for c in range(3):
            specs.append(pl.BlockSpec((1, d, qn, SW_WIDTH), lambda b, t, c=c: (b, 0, t, c)))
            args.append(qb)
        for c in (1, 2):
            for pi in range(npiece):
                back, per = npiece - pi, qn // sb
                specs.append(pl.BlockSpec(
                    (1, d, sb, SW_WIDTH),
                    lambda b, t, c=c, back=back, per=per: (b, 0, jnp.maximum(t * per - back, 0), c)))
                args.append(qb)
    specs.append(pl.BlockSpec((1, rt, SW_WIDTH), lambda b, t: (b, t, 0)))
    args.append(zb)
    npair = SW_WIDTH // LANES
    return pl.pallas_call(
        functools.partial(_attn_kernel, dils=dils, rt=rt, blk=blk),
        grid=(B, L // rt),
        in_specs=specs,
        out_specs=pl.BlockSpec((1, rt, SW_WIDTH), lambda b, t: (b, t, 0)),
        out_shape=jax.ShapeDtypeStruct((B, L, SW_WIDTH), BF16),
        scratch_shapes=[pltpu.VMEM((len(dils), npair, rt, LANES), F32),
                        pltpu.VMEM((len(dils), npair, rt, LANES), F32)],
        compiler_params=pltpu.CompilerParams(
            dimension_semantics=("arbitrary", "arbitrary"), vmem_limit_bytes=VMEM_LIMIT),
        name="attn_prompt",
    )(*args)


def _shift_window(src_ref, u, new_rows, new_scr, dst_ref):
    T = new_rows.shape[0]
    ncol = src_ref.shape[-1] // LANES
    shift = LANES - T
    new_scr[0:T, :] = new_rows
    new_t = new_scr[...].T
    lane = lax.broadcasted_iota(jnp.int32, (src_ref.shape[1], LANES), 1)
    cur = pltpu.roll(src_ref[u, :, 0:LANES], shift, 1)
    for j in range(ncol):
        if j + 1 < ncol:
            nxt = pltpu.roll(src_ref[u, :, (j + 1) * LANES:(j + 2) * LANES], shift, 1)
        else:
            nxt = pltpu.roll(new_t, shift, 1)
        dst_ref[u, :, j * LANES:(j + 1) * LANES] = jnp.where(lane < shift, cur, nxt)
        cur = nxt


def _sattn_kernel(qkv_ref, zb_ref, kt_ref, vt_ref, ob_ref, knew_scr, vnew_scr, *, T, W, dil):
    R = T * SW_HEADS
    ncol = W // LANES

    @pl.when(pl.program_id(0) == 0)
    def _():
        knew_scr[...] = jnp.zeros_like(knew_scr)
        vnew_scr[...] = jnp.zeros_like(vnew_scr)

    qkv = qkv_ref[0]
    knew_scr[0:T, :] = qkv[:, SW_WIDTH:2 * SW_WIDTH]
    vnew_scr[0:T, :] = qkv[:, 2 * SW_WIDTH:]
    knew = knew_scr[...]
    vnew = vnew_scr[...]

    head_of_lane = lax.broadcasted_iota(jnp.int32, (SW_HEADS, SW_WIDTH), 1) >> (SW_HD.bit_length() - 1)
    head_mask = head_of_lane == lax.broadcasted_iota(jnp.int32, (SW_HEADS, SW_WIDTH), 0)
    qbd = jnp.concatenate(
        [jnp.where(head_mask, jnp.broadcast_to(qkv[t:t + 1, 0:SW_WIDTH], (SW_HEADS, SW_WIDTH)), 0.0)
         for t in range(T)], axis=0).astype(BF16)
    hm_rows = jnp.concatenate([head_mask] * T, axis=0)

    scale = SW_HD ** -0.5

    s_cols = [_dot(qbd, kt_ref[0, :, j * LANES:(j + 1) * LANES].astype(BF16)) * scale for j in range(ncol)]
    s_cols.append(_dot_nt(qbd, knew.astype(BF16)) * scale)
    s_all = jnp.concatenate(s_cols, axis=1)

    wtot = W + LANES
    j_idx = lax.broadcasted_iota(jnp.int32, (R, wtot), 1)
    t_idx = lax.broadcasted_iota(jnp.int32, (R, wtot), 0) >> (SW_HEADS.bit_length() - 1)
    dist = jnp.where(j_idx < W, W + t_idx - j_idx, t_idx - (j_idx - W))
    exists = (j_idx < W + T) & (dist >= 0)

    ms, ls, ps = [], [], []
    for win, d in dil:
        valid = exists & ((dist & (d - 1)) == 0) & (dist <= win)
        sb = jnp.where(valid, s_all, -jnp.inf)
        m = jnp.max(sb, axis=-1, keepdims=True)
        pe = jnp.exp(sb - m)
        ms.append(m)
        ls.append(jnp.sum(pe, axis=-1, keepdims=True))
        ps.append(pe.astype(BF16))
    p_all = jnp.concatenate(ps, axis=0)
    num = _dot(p_all[:, W:], vnew.astype(BF16))
    for j in range(ncol):
        cols = slice(j * LANES, (j + 1) * LANES)
        num = num + _dot_nt(p_all[:, cols], vt_ref[0, :, cols].astype(BF16))
    mx = jnp.maximum(jnp.maximum(ms[0], ms[1]), ms[2])
    wts = [jnp.exp(m - mx) for m in ms]
    numer = sum(wts[i] * num[i * R:(i + 1) * R] for i in range(len(dil)))
    denom = sum(wts[i] * ls[i] for i in range(len(dil)))
    o = jnp.where(hm_rows, numer / denom, 0.0)
    zb = zb_ref[0]
    gate = _silu(zb)
    for t in range(T):
        row = jnp.sum(o[t * SW_HEADS:(t + 1) * SW_HEADS], axis=0, keepdims=True)
        ob_ref[0, t:t + 1, :] = row * gate[t:t + 1]


def _sattn(qkvb, zb, kt, vt):
    B, T, _ = qkvb.shape
    W = kt.shape[-1]
    kern = functools.partial(_sattn_kernel, T=T, W=W, dil=DILATIONS)
    small = lambda c: pl.BlockSpec((1, T, c), lambda b: (b, 0, 0))
    big = pl.BlockSpec((1, SW_WIDTH, W), lambda b: (b, 0, 0))
    return pl.pallas_call(
        kern,
        grid=(B,),
        in_specs=[small(3 * SW_WIDTH), small(SW_WIDTH), big, big],
        out_specs=small(SW_WIDTH),
        out_shape=jax.ShapeDtypeStruct((B, T, SW_WIDTH), F32),
        scratch_shapes=[pltpu.VMEM((LANES, SW_WIDTH), F32), pltpu.VMEM((LANES, SW_WIDTH), F32)],
        compiler_params=pltpu.CompilerParams(
            dimension_semantics=("arbitrary",), vmem_limit_bytes=VMEM_LIMIT),
        name="sattn",
    )(qkvb, zb, kt, vt)


def _out_kernel(x_ref, oa_ref, ob_ref, w_ref, fg_ref, y_ref):
    acc = _dot(oa_ref[0], w_ref[0:DN_WIDTH, :]) + _dot(ob_ref[0], w_ref[DN_WIDTH:, :])
    y = x_ref[0] + acc
    y_ref[0] = (y * lax.rsqrt(jnp.mean(y * y, axis=-1, keepdims=True) + NORM_EPS)) * fg_ref[...]


def _out(x, oa, ob, w, fg, *, tm, name):
    B, L, _ = x.shape
    row = lambda b, t: (b, t, 0)
    return pl.pallas_call(
        _out_kernel,
        grid=(B, L // tm),
        in_specs=[pl.BlockSpec((1, tm, D_MODEL), row), pl.BlockSpec((1, tm, DN_WIDTH), row),
                  pl.BlockSpec((1, tm, SW_WIDTH), row),
                  _const_spec((DN_WIDTH + SW_WIDTH, D_MODEL)), _const_spec((1, D_MODEL))],
        out_specs=pl.BlockSpec((1, tm, D_MODEL), row),
        out_shape=jax.ShapeDtypeStruct((B, L, D_MODEL), F32),
        compiler_params=pltpu.CompilerParams(
            dimension_semantics=("arbitrary", "arbitrary"), vmem_limit_bytes=VMEM_LIMIT),
        name=name,
    )(x, oa, ob, w, fg)


def _pad_lanes(v):
    return jnp.pad(v.astype(F32), (0, LANES - v.shape[0])).reshape(1, LANES)


def _tile(n, pref):
    return pref if n % pref == 0 else n


def _layers(xp, xs, conv_state, s0, k_win, v_win, wp, g_row, conv_w, alog_row, dtb_row, ng_row,
            w_out16, fg_row):
    B, T, _ = xs.shape
    W = k_win.shape[1]
    hist = CONV_W - 1
    group = SUBLANES
    assert T + hist <= group
    rows = xs.reshape(1, B * T, D_MODEL)
    raw, za, zb, ab, qkvb = _proj_sample(rows, g_row, wp, tm=_tile(B * T, 256))
    raw = raw.reshape(B, T, CONV_CH)
    qkvb = qkvb.reshape(B, T, 3 * SW_WIDTH)
    front = lambda a: jnp.pad(a.reshape(B, T, -1), ((0, 0), (group - T, 0), (0, 0))).reshape(1, B * group, -1)
    ext = jnp.concatenate([jnp.zeros((B, group - T - hist, CONV_CH), F32), conv_state, raw], axis=1)
    qkvn, gb = _gdn_pre(ext.reshape(1, B * group, CONV_CH), front(ab), conv_w, alog_row, dtb_row,
                        group=group, t_new=T)
    kt = jnp.transpose(k_win, (0, 2, 3, 1)).reshape(B, SW_WIDTH, W)
    vt = jnp.transpose(v_win, (0, 2, 3, 1)).reshape(B, SW_WIDTH, W)

    Bp, L, _ = xp.shape
    keep = min(WIN_MAX, L)
    dils = tuple(d for _, d in DILATIONS)
    tm = _tile(L, 512)
    p_qkvn, p_gb, p_za, p_zb, raw_last, kv_last, qb1, qb4, qb16 = _proj_prompt(
        xp, g_row, wp, conv_w, alog_row, dtb_row, dils=dils, keep=keep, tm=tm)
    p_oa, p_s, kt_new, vt_new = _gdn_prompt(p_qkvn, p_gb, p_za, ng_row, kt, vt, qkvb,
                                            C=DN_CHUNK, TL=_tile(L, GDN_TILE))
    steps = {win // d for win, d in DILATIONS}
    assert len(steps) == 1, "every dilation looks back the same number of class rows"
    p_ob = _attn_prompt((qb1, qb4, qb16), p_zb, dils=dils, blk=steps.pop(), rt=_tile(L, 1024))
    yp = _out(xp, p_oa, p_ob, w_out16, fg_row, tm=tm, name="out_prompt")
    prompt = (yp, raw_last[:, HALO - hist:, :], p_s,
              kv_last[:, :, :SW_WIDTH].reshape(Bp, keep, SW_HEADS, SW_HD),
              kv_last[:, :, SW_WIDTH:].reshape(Bp, keep, SW_HEADS, SW_HD))

    oa, s_new = _gdn_sample(qkvn, gb, front(za), s0, ng_row, C=group, n=_tile(B * group, GDN_TILE))
    oa = oa.reshape(B, group, DN_WIDTH)[:, group - T:].reshape(1, B * T, DN_WIDTH)
    ob = _sattn(qkvb, zb.reshape(B, T, SW_WIDTH), kt, vt)
    ys = _out(rows, oa, ob.reshape(1, B * T, SW_WIDTH).astype(BF16), w_out16, fg_row,
              tm=_tile(B * T, 256), name="out_sample")
    sample = (ys.reshape(B, T, D_MODEL), jnp.concatenate([conv_state, raw], axis=1)[:, -hist:], s_new,
              jnp.transpose(kt_new.reshape(B, SW_HEADS, SW_HD, W), (0, 3, 1, 2)),
              jnp.transpose(vt_new.reshape(B, SW_HEADS, SW_HD, W), (0, 3, 1, 2)))
    return prompt, sample


def kernel(x_prompt, x_sample, state_conv, state_delta, cache_k_win, cache_v_win, norm_g, final_norm_g,
           w_in, conv_w, a_log, dt_bias, dn_norm_g, w_out):
    assert norm_g.shape[0] == 1, "single layer"
    w = w_in[0]
    o1 = CONV_CH + DN_WIDTH
    o3 = o1 + 2 * DN_HEADS
    o4 = o3 + 3 * SW_WIDTH
    wp = jnp.concatenate(
        [w[:, :o1], w[:, o3:], w[:, o1:o3], jnp.zeros((D_MODEL, LANES - 2 * DN_HEADS), w.dtype)],
        axis=1).astype(BF16)
    assert wp.shape[1] == W_COLS and o4 - o3 == 3 * SW_WIDTH
    g_row = norm_g[0].reshape(1, D_MODEL)
    fg_row = final_norm_g.reshape(1, D_MODEL)
    alog_row = _pad_lanes(a_log[0])
    dtb_row = _pad_lanes(dt_bias[0])
    ng_row = dn_norm_g[0].reshape(1, DN_DV)
    w_out16 = w_out[0].astype(BF16)
    cw = conv_w[0]
    common = (wp, g_row, cw, alog_row, dtb_row, ng_row, w_out16, fg_row)
    (yp, pc, ps, pk, pv), (ys, sc, ss, sk, sv) = _layers(
        x_prompt, x_sample, state_conv[0], state_delta[0], cache_k_win[0], cache_v_win[0], *common)
    return (yp, ys, pc[None], ps[None], pk[None], pv[None], sc[None], ss[None], sk[None], sv[None])
```

```python
import functools

import numpy as np
import jax
import jax.numpy as jnp
from jax import lax
from jax.experimental import pallas as pl
from jax.experimental.pallas import tpu as pltpu

F32 = jnp.float32
BF16 = jnp.bfloat16

D_MODEL = 1024
DN_HEADS = 4
DN_DK = 128
DN_DV = 128
QK_W = DN_HEADS * DN_DK
DN_WIDTH = DN_HEADS * DN_DV
CONV_W = 4
CONV_CH = 2 * QK_W + DN_WIDTH
DN_CHUNK = 64
SW_HEADS = 8
SW_HD = 64
SW_WIDTH = SW_HEADS * SW_HD
DILATIONS = ((128, 1), (512, 4), (2048, 16))
WIN_MAX = 2048
NORM_EPS = 1e-6

LANES = 128
SUBLANES = 8
VMEM_LIMIT = 56 * 1024 * 1024

C_RAW = 0
C_ZA = CONV_CH
C_QKVB = C_ZA + DN_WIDTH
C_ZB = C_QKVB + 3 * SW_WIDTH
C_AB = C_ZB + SW_WIDTH
W_COLS = C_AB + LANES
HALO = SUBLANES
GDN_TILE = DN_HEADS * DN_CHUNK


def _dot(a, b):
    return lax.dot_general(a, b, (((a.ndim - 1,), (0,)), ((), ())), preferred_element_type=F32)


def _dot_nt(a, b):
    return lax.dot_general(a, b, (((1,), (1,)), ((), ())), preferred_element_type=F32)


def _dot_tn(a, b):
    return lax.dot_general(a, b, (((0,), (0,)), ((), ())), preferred_element_type=F32)


def _bdot(a, b):
    return lax.dot_general(a, b, (((2,), (1,)), ((0,), (0,))), preferred_element_type=F32)


def _bdot_tn(a, b):
    return lax.dot_general(a, b, (((1,), (1,)), ((0,), (0,))), preferred_element_type=F32)


def _dot_exact_lhs(a16, x):
    hi = x.astype(BF16)
    r1 = x - hi.astype(F32)
    mid = r1.astype(BF16)
    lo = (r1 - mid.astype(F32)).astype(BF16)
    return _dot(a16, lo) + _dot(a16, mid) + _dot(a16, hi)


def _sigmoid(x):
    return 1.0 / (1.0 + jnp.exp(-x))


def _silu(x):
    return x * _sigmoid(x)


def _conv_norm_gates(ext_scr, cw_ref, alog_ref, dtb_ref, qkvn_ref, gb_ref, rows, live_from):
    ch = LANES if rows % LANES == 0 else rows
    cw = cw_ref[...]
    a_neg = -jnp.exp(alog_ref[...])
    dtb = dtb_ref[...]
    lane = lax.broadcasted_iota(jnp.int32, (ch, LANES), 1)

    def chunk(c, carry):
        r0 = c * ch
        out = pl.ds(r0, ch)
        for s in range(CONV_CH // LANES):
            sl = slice(s * LANES, (s + 1) * LANES)
            win = ext_scr[pl.ds(r0, ch + HALO), sl]
            y = cw[0:1, sl] * win[HALO - 3:HALO - 3 + ch]
            y = y + cw[1:2, sl] * win[HALO - 2:HALO - 2 + ch]
            y = y + cw[2:3, sl] * win[HALO - 1:HALO - 1 + ch]
            y = y + cw[3:4, sl] * win[HALO:HALO + ch]
            v = _silu(y)
            if s < DN_HEADS:
                v = v * lax.rsqrt(jnp.sum(v * v, -1, keepdims=True) + NORM_EPS) * (DN_DK ** -0.5)
            elif s < 2 * DN_HEADS:
                v = v * lax.rsqrt(jnp.sum(v * v, -1, keepdims=True) + NORM_EPS)
            qkvn_ref[0, out, sl] = v.astype(qkvn_ref.dtype)
        ab = gb_ref[0, out, :]
        xg = ab + dtb
        softplus = jnp.maximum(xg, 0.0) + jnp.log1p(jnp.exp(-jnp.abs(xg)))
        gb = jnp.where(lane < DN_HEADS, a_neg * softplus, _sigmoid(ab))
        if live_from is not None:
            group, first = live_from
            rowi = lax.broadcasted_iota(jnp.int32, (ch, LANES), 0) + r0
            gb = jnp.where((rowi & (group - 1)) >= first, gb, 0.0)
        gb_ref[0, out, :] = gb
        return carry

    for c in range(rows // ch):
        chunk(c, 0)


def _normed(x_ref, g_ref, h_scr):
    rows = x_ref.shape[1]
    ch = LANES if rows % LANES == 0 else rows
    g = g_ref[...]

    for c in range(rows // ch):
        rws = pl.ds(c * ch, ch)
        x = x_ref[0, rws, :]
        ms = jnp.mean(x * x, axis=-1, keepdims=True)
        h_scr[rws, :] = ((x * lax.rsqrt(ms + NORM_EPS)) * g).astype(BF16)
    return h_scr[...]


def _proj_sample_kernel(x_ref, g_ref, w_ref, raw_ref, za_ref, zb_ref, ab_ref, qb_ref, h_scr):
    h = _normed(x_ref, g_ref, h_scr)
    raw_ref[0] = _dot(h, w_ref[:, C_RAW:C_ZA])
    za_ref[0] = _dot(h, w_ref[:, C_ZA:C_QKVB])
    zb_ref[0] = _dot(h, w_ref[:, C_ZB:C_AB])
    ab_ref[0] = _dot(h, w_ref[:, C_AB:W_COLS])
    qb_ref[0] = _dot(h, w_ref[:, C_QKVB:C_ZB])


def _proj_prompt_kernel(x_ref, g_ref, w_ref, cw_ref, alog_ref, dtb_ref,
                        qkvn_ref, gb_ref, za_ref, zb_ref, rawlast_ref, kv_ref, *rest, dils, tm):
    nd = len(dils)
    qb_refs = rest[:nd]
    cls_scrs = list(rest[nd:2 * nd - 1]) + [None]
    ext_scr, h_scr = rest[2 * nd - 1:]

    @pl.when(pl.program_id(1) == 0)
    def _():
        ext_scr[0:HALO, :] = jnp.zeros((HALO, CONV_CH), F32)

    _normed(x_ref, g_ref, h_scr)
    qkvb = _dot(h_scr[...], w_ref[:, C_QKVB:C_ZB])
    ext_scr[HALO:HALO + tm, :] = _dot(h_scr[...], w_ref[:, C_RAW:C_ZA])
    za_ref[0] = _dot(h_scr[...], w_ref[:, C_ZA:C_QKVB]).astype(za_ref.dtype)
    zb_ref[0] = _dot(h_scr[...], w_ref[:, C_ZB:C_AB]).astype(zb_ref.dtype)
    gb_ref[0] = _dot(h_scr[...], w_ref[:, C_AB:W_COLS])

    kv_ref[0] = qkvb[:, SW_WIDTH:]
    nslab = 3 * SW_WIDTH // LANES
    assert dils[0] == 1
    qb_refs[0][0, 0] = qkvb.astype(BF16)
    for j in range(nslab):
        cls_scrs[0][j] = qkvb[:, j * LANES:(j + 1) * LANES]
    for i in range(1, nd):
        d, prev = dils[i], dils[i - 1]
        ratio, n_d, n_prev = d // prev, tm // d, tm // prev
        assert d % prev == 0
        for r in range(d):
            for j in range(nslab):
                v = cls_scrs[i - 1][j, pl.ds((r % prev) * n_prev + r // prev, n_d, stride=ratio), :]
                qb_refs[i][0, r, :, j * LANES:(j + 1) * LANES] = v.astype(BF16)
                if cls_scrs[i] is not None:
                    cls_scrs[i][j, r * n_d:(r + 1) * n_d, :] = v

    _conv_norm_gates(ext_scr, cw_ref, alog_ref, dtb_ref, qkvn_ref, gb_ref, tm, None)
    last = ext_scr[tm:tm + HALO, :]
    rawlast_ref[0] = last
    ext_scr[0:HALO, :] = last


def _const_spec(shape):
    zeros = (0,) * len(shape)
    return pl.BlockSpec(shape, lambda *_: zeros)


def _proj_prompt(x, g, w, cw, alog_row, dtb_row, *, dils, keep, tm):
    B, L, _ = x.shape
    row = lambda b, t: (b, t, 0)
    first = (L - keep) // tm
    out_shape = [jax.ShapeDtypeStruct((B, L, CONV_CH), F32),
                 jax.ShapeDtypeStruct((B, L, LANES), F32),
                 jax.ShapeDtypeStruct((B, L, DN_WIDTH), BF16),
                 jax.ShapeDtypeStruct((B, L, SW_WIDTH), BF16),
                 jax.ShapeDtypeStruct((B, HALO, CONV_CH), F32),
                 jax.ShapeDtypeStruct((B, keep, 2 * SW_WIDTH), F32)]
    out_specs = [pl.BlockSpec((1, tm, CONV_CH), row),
                 pl.BlockSpec((1, tm, LANES), row),
                 pl.BlockSpec((1, tm, DN_WIDTH), row),
                 pl.BlockSpec((1, tm, SW_WIDTH), row),
                 pl.BlockSpec((1, HALO, CONV_CH), lambda b, t: (b, 0, 0)),
                 pl.BlockSpec((1, tm, 2 * SW_WIDTH), lambda b, t: (b, jnp.maximum(t - first, 0), 0))]
    for d in dils:
        out_shape.append(jax.ShapeDtypeStruct((B, d, L // d, 3 * SW_WIDTH), BF16))
        out_specs.append(pl.BlockSpec((1, d, tm // d, 3 * SW_WIDTH), lambda b, t: (b, 0, t, 0)))
    return pl.pallas_call(
        functools.partial(_proj_prompt_kernel, dils=dils, tm=tm),
        grid=(B, L // tm),
        in_specs=[pl.BlockSpec((1, tm, D_MODEL), row),
                  _const_spec((1, D_MODEL)),
                  pl.BlockSpec((D_MODEL, W_COLS), lambda b, t: (0, 0), pipeline_mode=pl.Buffered(1)),
                  _const_spec((CONV_W, CONV_CH)),
                  _const_spec((1, LANES)),
                  _const_spec((1, LANES))],
        out_specs=out_specs,
        out_shape=out_shape,
        scratch_shapes=[pltpu.VMEM((3 * SW_WIDTH // LANES, tm, LANES), F32)] * (len(dils) - 1)
        + [pltpu.VMEM((tm + HALO, CONV_CH), F32), pltpu.VMEM((tm, D_MODEL), BF16)],
        compiler_params=pltpu.CompilerParams(
            dimension_semantics=("arbitrary", "arbitrary"), vmem_limit_bytes=VMEM_LIMIT),
        name="proj_prompt",
    )(x, g, w, cw, alog_row, dtb_row)


def _proj_sample(x, g, w, *, tm):
    B, L, _ = x.shape
    row = lambda b, t: (b, t, 0)
    widths = (CONV_CH, DN_WIDTH, SW_WIDTH, LANES, 3 * SW_WIDTH)
    return pl.pallas_call(
        _proj_sample_kernel,
        grid=(B, L // tm),
        in_specs=[pl.BlockSpec((1, tm, D_MODEL), row), _const_spec((1, D_MODEL)),
                  _const_spec((D_MODEL, W_COLS))],
        out_specs=[pl.BlockSpec((1, tm, c), row) for c in widths],
        out_shape=[jax.ShapeDtypeStruct((B, L, c), F32) for c in widths],
        scratch_shapes=[pltpu.VMEM((tm, D_MODEL), BF16)],
        compiler_params=pltpu.CompilerParams(
            dimension_semantics=("arbitrary", "arbitrary"), vmem_limit_bytes=VMEM_LIMIT),
        name="proj_sample",
    )(x, g, w)


def _gdn_pre_kernel(ext_ref, ab_ref, cw_ref, alog_ref, dtb_ref, qkvn_ref, gb_ref, ext_scr, *, rows, group, t_new):
    ext_scr[0:HALO, :] = jnp.zeros((HALO, CONV_CH), F32)
    ext_scr[HALO:HALO + rows, :] = ext_ref[0]
    gb_ref[0] = ab_ref[0]
    _conv_norm_gates(ext_scr, cw_ref, alog_ref, dtb_ref, qkvn_ref, gb_ref, rows, (group, group - t_new))


def _gdn_pre(ext, ab, cw, alog_row, dtb_row, *, group, t_new):
    _, rows, _ = ext.shape
    return pl.pallas_call(
        functools.partial(_gdn_pre_kernel, rows=rows, group=group, t_new=t_new),
        grid=(1,),
        in_specs=[_const_spec((1, rows, CONV_CH)), _const_spec((1, rows, LANES)),
                  _const_spec((CONV_W, CONV_CH)), _const_spec((1, LANES)), _const_spec((1, LANES))],
        out_specs=[_const_spec((1, rows, CONV_CH)), _const_spec((1, rows, LANES))],
        out_shape=[jax.ShapeDtypeStruct((1, rows, CONV_CH), F32), jax.ShapeDtypeStruct((1, rows, LANES), F32)],
        scratch_shapes=[pltpu.VMEM((rows + HALO, CONV_CH), F32)],
        compiler_params=pltpu.CompilerParams(
            dimension_semantics=("arbitrary",), vmem_limit_bytes=VMEM_LIMIT),
        name="gdn_pre_sample",
    )(ext, ab, cw, alog_row, dtb_row)


def _block_masks(n, c):
    r = np.arange(n)[:, None]
    cc = np.arange(n)[None, :]
    x = r ^ cc
    same = x < c
    incl = same & (cc <= r)
    levels = [(x >> 1) == 0]
    k = 1
    while (2 << k) <= c:
        levels.append((x >> k) == 1)
        k += 1
    masks = np.stack([r == cc, incl] + levels).astype(np.float32)
    lmask = np.concatenate([incl, same], axis=0).astype(np.float32)
    return jnp.asarray(masks), jnp.asarray(lmask, dtype=BF16)


def _gdn_tiles(tiles, masks_ref, lmask_ref):
    n = tiles[0][0].shape[0]
    nt = len(tiles)
    nlev = masks_ref.shape[0] - 2
    eye = masks_ref[0]
    incl = masks_ref[1]
    g_b = [jnp.broadcast_to(t[3], (n, LANES)) for t in tiles]
    beta_b = [jnp.broadcast_to(t[4], (n, LANES)) for t in tiles]
    cs = _dot_exact_lhs(lmask_ref[...], jnp.concatenate(g_b, axis=1))
    gc_b = [cs[:n, i * LANES:(i + 1) * LANES] for i in range(nt)]
    gl_b = [cs[n:, i * LANES:(i + 1) * LANES] for i in range(nt)]
    kb = [tiles[i][1] * beta_b[i] for i in range(nt)]
    k16 = [t[1].astype(BF16) for t in tiles]
    kk = [_dot_nt(kb[i].astype(BF16), k16[i]) for i in range(nt)]
    qk = [_dot_nt(tiles[i][0].astype(BF16), k16[i]) for i in range(nt)]
    a, attn = [], []
    for i in range(nt):
        gc_sq = jnp.concatenate([gc_b[i]] * (n // LANES), axis=1)
        gc_row = jnp.sum(gc_sq * eye, axis=0, keepdims=True)
        dec = jnp.exp(jnp.minimum(gc_sq - gc_row, 0.0)) * incl
        a.append(kk[i] * (dec - eye))
        attn.append(qk[i] * dec)
    t = [eye - a[i] * masks_ref[2] for i in range(nt)]
    for lv in range(1, nlev):
        t16 = [x.astype(BF16) for x in t]
        p = [_dot(t16[i], (a[i] * masks_ref[2 + lv]).astype(BF16)) for i in range(nt)]
        t = [t[i] - _dot(p[i].astype(BF16), t16[i]) for i in range(nt)]
    e_gc = [jnp.exp(x) for x in gc_b]
    sol = [_dot(t[i].astype(BF16),
                jnp.concatenate([tiles[i][2] * beta_b[i], kb[i] * e_gc[i]], axis=1).astype(BF16))
           for i in range(nt)]
    return [(sol[i][:, :DN_DV], sol[i][:, DN_DV:], tiles[i][0] * e_gc[i],
             tiles[i][1] * jnp.exp(gl_b[i] - gc_b[i]), attn[i], gl_b[i]) for i in range(nt)]


def _out_norm(o, ng, za):
    return ((o * lax.rsqrt(jnp.mean(o * o, -1, keepdims=True) + NORM_EPS)) * ng) * _silu(za.astype(F32))


def _gdn_prompt_kernel(qkvn_ref, gb_ref, za_ref, ng_ref, masks_ref, lmask_ref, kt_ref, vt_ref, new_ref,
                       zbs_ref, o_ref, s_out_ref, kto_ref, vto_ref, obs_ref, s_scr, knew_scr, vnew_scr,
                       *, C, TL, G):
    H = DN_HEADS

    @pl.when(pl.program_id(1) == 0)
    def _():
        s_scr[...] = jnp.zeros_like(s_scr)

    @pl.when((pl.program_id(0) == 0) & (pl.program_id(1) == 0))
    def _():
        knew_scr[...] = jnp.zeros_like(knew_scr)
        vnew_scr[...] = jnp.zeros_like(vnew_scr)

    for u in range(kt_ref.shape[0]):
        _shift_window(kt_ref, u, new_ref[u, :, SW_WIDTH:2 * SW_WIDTH], knew_scr, kto_ref)
        _shift_window(vt_ref, u, new_ref[u, :, 2 * SW_WIDTH:], vnew_scr, vto_ref)
        _sample_attention(new_ref[u], zbs_ref[u], kt_ref, vt_ref, u, knew_scr[...], vnew_scr[...],
                          obs_ref, DILATIONS)

    ng = ng_ref[...]

    def group(gi, carry):
        base = gi * (G * C) if isinstance(gi, int) else pl.multiple_of(gi * (G * C), G * C)
        rows = [pl.ds(base + i * C, C) for i in range(G)]

        def stack(ref, rws, off):
            return jnp.concatenate(
                [ref[0, rws, off + h * LANES:off + (h + 1) * LANES] for h in range(H)], axis=0)

        tiles, zas = [], []
        for rws in rows:
            gbc = gb_ref[0, rws, :]
            g_col = jnp.concatenate([gbc[:, h:h + 1] for h in range(H)], axis=0)
            beta_col = jnp.concatenate([gbc[:, H + h:H + h + 1] for h in range(H)], axis=0)
            zas.append(stack(za_ref, rws, 0))
            tiles.append((stack(qkvn_ref, rws, 0).astype(F32), stack(qkvn_ref, rws, QK_W).astype(F32),
                          stack(qkvn_ref, rws, 2 * QK_W).astype(F32), g_col, beta_col))
        tiles = _gdn_tiles(tiles, masks_ref, lmask_ref)
        s = [s_scr[h] for h in range(H)]
        outs = []
        for (u, w, q_dec, k_dec, attn, gl_b), za in zip(tiles, zas):
            ws, qs = [], []
            for h in range(H):
                blk = slice(h * C, (h + 1) * C)
                r = _dot(jnp.concatenate([w[blk], q_dec[blk]], axis=0).astype(BF16), s[h].astype(BF16))
                ws.append(r[:C])
                qs.append(r[C:])
            v_new = u - jnp.concatenate(ws, axis=0)
            vn16 = v_new.astype(BF16)
            o = jnp.concatenate(qs, axis=0) + _dot(attn.astype(BF16), vn16)
            for h in range(H):
                blk = slice(h * C, (h + 1) * C)
                gl_s = jnp.broadcast_to(jnp.exp(gl_b[h * C:h * C + 1, :]), (DN_DK, DN_DV))
                s[h] = s[h] * gl_s + _dot_tn(k_dec[blk].astype(BF16), vn16[blk])
            outs.append(_out_norm(o, ng, za))
        for rws, out in zip(rows, outs):
            for h in range(H):
                o_ref[0, rws, h * DN_DV:(h + 1) * DN_DV] = out[h * C:(h + 1) * C].astype(o_ref.dtype)
        for h in range(H):
            s_scr[h] = s[h]
        return carry

    for gi in range(TL // (G * C)):
        group(gi, 0)
    s_out_ref[0] = s_scr[...]


def _gdn_prompt(qkvn, gb, za, ng_row, kt, vt, new_rows, zbs, *, C, TL):
    B, L, _ = qkvn.shape
    n = DN_HEADS * C
    nt = L // TL
    Bs, T, _ = new_rows.shape
    W = kt.shape[-1]
    assert Bs % (B * nt) == 0
    per = Bs // (B * nt)
    masks, lmask = _block_masks(n, C)
    row = lambda b, t: (b, t, 0)
    share = lambda b, t: (b * nt + t, 0, 0)
    window = pl.BlockSpec((per, SW_WIDTH, W), share)
    return pl.pallas_call(
        functools.partial(_gdn_prompt_kernel, C=C, TL=TL, G=4 if TL % (4 * C) == 0 else 1),
        grid=(B, nt),
        in_specs=[pl.BlockSpec((1, TL, CONV_CH), row),
                  pl.BlockSpec((1, TL, LANES), row),
                  pl.BlockSpec((1, TL, DN_WIDTH), row),
                  _const_spec((1, DN_DV)), _const_spec(masks.shape), _const_spec(lmask.shape),
                  window, window, pl.BlockSpec((per, T, 3 * SW_WIDTH), share),
                  pl.BlockSpec((per, T, SW_WIDTH), share)],
        out_specs=[pl.BlockSpec((1, TL, DN_WIDTH), row),
                   pl.BlockSpec((1, DN_HEADS, DN_DK, DN_DV), lambda b, t: (b, 0, 0, 0)),
                   window, window, pl.BlockSpec((per, T, SW_WIDTH), share)],
        out_shape=[jax.ShapeDtypeStruct((B, L, DN_WIDTH), BF16),
                   jax.ShapeDtypeStruct((B, DN_HEADS, DN_DK, DN_DV), F32),
                   jax.ShapeDtypeStruct(kt.shape, F32), jax.ShapeDtypeStruct(vt.shape, F32),
                   jax.ShapeDtypeStruct((Bs, T, SW_WIDTH), F32)],
        scratch_shapes=[pltpu.VMEM((DN_HEADS, DN_DK, DN_DV), F32),
                        pltpu.VMEM((LANES, SW_WIDTH), F32), pltpu.VMEM((LANES, SW_WIDTH), F32)],
        compiler_params=pltpu.CompilerParams(
            dimension_semantics=("arbitrary", "arbitrary"), vmem_limit_bytes=VMEM_LIMIT),
        name="gdn_prompt",
    )(qkvn, gb, za, ng_row, masks, lmask, kt, vt, new_rows, zbs)


def _gdn_sample_kernel(qkvn_ref, gb_ref, za_ref, s0_ref, ng_ref, masks_ref, lmask_ref, o_ref, s_out_ref, *, C):
    n = qkvn_ref.shape[1]
    nb = n // C
    ng = ng_ref[...]
    split = lambda a: a.reshape(nb, C, a.shape[-1])
    tiles = _gdn_tiles(
        [(qkvn_ref[0, :, h * LANES:(h + 1) * LANES],
          qkvn_ref[0, :, QK_W + h * LANES:QK_W + (h + 1) * LANES],
          qkvn_ref[0, :, 2 * QK_W + h * LANES:2 * QK_W + (h + 1) * LANES],
          gb_ref[0, :, h:h + 1], gb_ref[0, :, DN_HEADS + h:DN_HEADS + h + 1]) for h in range(DN_HEADS)],
        masks_ref, lmask_ref)
    for h in range(DN_HEADS):
        sl = slice(h * LANES, (h + 1) * LANES)
        u, w, q_dec, k_dec, attn, gl_b = tiles[h]
        s0 = s0_ref[:, h]
        r = _bdot(jnp.concatenate([split(w), split(q_dec)], axis=1).astype(BF16), s0.astype(BF16))
        v_new = split(u) - r[:, :C]
        vn16 = v_new.astype(BF16)
        o = r[:, C:].reshape(n, DN_DV) + _dot(attn.astype(BF16), vn16.reshape(n, DN_DV))
        gl = split(jnp.exp(gl_b))[:, 0:1, :]
        s_out_ref[:, h] = s0 * gl + _bdot_tn(split(k_dec).astype(BF16), vn16)
        o_ref[0, :, sl] = _out_norm(o, ng, za_ref[0, :, sl]).astype(o_ref.dtype)


def _gdn_sample(qkvn, gb, za, s0, ng_row, *, C, n):
    _, rows, _ = qkvn.shape
    nb = n // C
    masks, lmask = _block_masks(n, C)
    row = lambda i: (0, i, 0)
    state = pl.BlockSpec((nb, DN_HEADS, DN_DK, DN_DV), lambda i: (i, 0, 0, 0))
    return pl.pallas_call(
        functools.partial(_gdn_sample_kernel, C=C),
        grid=(rows // n,),
        in_specs=[pl.BlockSpec((1, n, CONV_CH), row), pl.BlockSpec((1, n, LANES), row),
                  pl.BlockSpec((1, n, DN_WIDTH), row), state,
                  _const_spec((1, DN_DV)), _const_spec(masks.shape), _const_spec(lmask.shape)],
        out_specs=[pl.BlockSpec((1, n, DN_WIDTH), row), state],
        out_shape=[jax.ShapeDtypeStruct((1, rows, DN_WIDTH), BF16), jax.ShapeDtypeStruct(s0.shape, F32)],
        compiler_params=pltpu.CompilerParams(
            dimension_semantics=("arbitrary",), vmem_limit_bytes=VMEM_LIMIT),
        name="gdn_sample",
    )(qkvn, gb, za, s0, ng_row, masks, lmask)


def _swa_blocks(items):
    mq = items[0][0].shape[0]
    npair = SW_WIDTH // LANES
    nh = 2 * npair
    lo_half = lax.broadcasted_iota(jnp.int32, (mq, LANES), 1) < SW_HD
    scale = SW_HD ** -0.5
    half_masks = (jnp.where(lo_half, scale, 0.0).astype(BF16), jnp.where(lo_half, 0.0, scale).astype(BF16))
    scores = []
    for q, kk, _, _ in items:
        for p in range(npair):
            sl = slice(p * LANES, (p + 1) * LANES)
            for half in range(2):
                scores.append(_dot_nt(q[:, sl] * half_masks[half], kk[:, sl]))
    probs, inv_l, lses = [], [], []
    for i, s in enumerate(scores):
        s = jnp.where(items[i // nh][3], s, -jnp.inf)
        m = jnp.max(s, axis=-1, keepdims=True)
        pe = jnp.exp(s - m)
        l = jnp.sum(pe, axis=-1, keepdims=True)
        probs.append(pe.astype(BF16))
        inv_l.append(1.0 / l)
        lses.append(jnp.broadcast_to(m + jnp.log(l), (mq, LANES)))
    pvs = [_dot(probs[i], items[i // nh][2][:, ((i % nh) // 2) * LANES:((i % nh) // 2 + 1) * LANES])
           for i in range(len(scores))]
    out = []
    for b in range(len(items)):
        e = [b * nh + 2 * p for p in range(npair)]
        out.append([(jnp.where(lo_half, pvs[i] * inv_l[i], pvs[i + 1] * inv_l[i + 1]),
                     jnp.where(lo_half, lses[i], lses[i + 1])) for i in e])
    return out


def _swa_geometry(rt, d, blk):
    qn = rt // d
    sb = min(blk, qn)
    return qn, sb, blk // sb


def _attn_kernel(*refs, dils, rt, blk):
    nbr = len(dils)
    zb_ref, ob_ref, o_scr, lse_scr = refs[-4:]
    t = pl.program_id(1)
    npair = SW_WIDTH // LANES
    idx = 0
    for bi, d in enumerate(dils):
        qn, sb, npiece = _swa_geometry(rt, d, blk)
        q_ref, k_ref, v_ref = refs[idx:idx + 3]
        kp = refs[idx + 3:idx + 3 + npiece]
        vp = refs[idx + 3 + npiece:idx + 3 + 2 * npiece]
        idx += 3 + 2 * npiece
        nk = blk + sb
        qi = lax.broadcasted_iota(jnp.int32, (sb, nk), 0)
        kj = lax.broadcasted_iota(jnp.int32, (sb, nk), 1)
        band = (kj >= qi) & (kj <= qi + blk)
        band0 = band & (kj >= blk - t * qn)

        nsub = qn // sb

        def first(r, q_ref=q_ref, k_ref=k_ref, v_ref=v_ref, kp=kp, vp=vp, sb=sb, band0=band0):
            kk = jnp.concatenate([x[0, r] for x in kp] + [k_ref[0, r, 0:sb, :]], axis=0)
            vv = jnp.concatenate([x[0, r] for x in vp] + [v_ref[0, r, 0:sb, :]], axis=0)
            return r, 0, (q_ref[0, r, 0:sb, :], kk, vv, band0)

        def later(r, j, q_ref=q_ref, k_ref=k_ref, v_ref=v_ref, sb=sb, band=band):
            aligned = (lambda x: x) if isinstance(j, int) else (lambda x: pl.multiple_of(x, sb))
            keys = pl.ds(aligned((j - 1) * sb), 2 * sb)
            return r, j, (q_ref[0, r, pl.ds(aligned(j * sb), sb), :],
                          k_ref[0, r, keys, :], v_ref[0, r, keys, :], band)

        def run(blocks, bi=bi, d=d, sb=sb):
            for (r, j, _), res in zip(blocks, _swa_blocks([b[2] for b in blocks])):
                start = j * (sb * d) + r
                rows = pl.ds(start, sb, stride=d) if d > 1 else pl.ds(start, sb)
                for p in range(npair):
                    o_scr[bi, p, rows, :] = res[p][0]
                    lse_scr[bi, p, rows, :] = res[p][1]

        if nsub == 1:
            gs = 4
            assert d % gs == 0
            lax.fori_loop(0, d // gs, lambda g, c, run=run, first=first:
                          (run([first(gs * g + i) for i in range(gs)]), c)[1], 0)
        else:
            assert nsub % 2 == 0

            def do_class(r, run=run, first=first, later=later, nsub=nsub):
                run([first(r), later(r, 1)])
                if nsub > 2:
                    lax.fori_loop(1, nsub // 2, lambda g, c:
                                  (run([later(r, 2 * g), later(r, 2 * g + 1)]), c)[1], 0)

            if d == 1:
                do_class(0)
            else:
                lax.fori_loop(0, d, lambda r, c, f=do_class: (f(r), c)[1], 0)

    def merge(c, carry):
        rows = pl.ds(pl.multiple_of(c * blk, blk), blk)
        zb = zb_ref[0, rows, :].astype(F32)
        for p in range(npair):
            lses = [lse_scr[i, p, rows, :] for i in range(nbr)]
            mx = functools.reduce(jnp.maximum, lses)
            wts = [jnp.exp(l - mx) for l in lses]
            numer = sum(wts[i] * o_scr[i, p, rows, :] for i in range(nbr))
            sl = slice(p * LANES, (p + 1) * LANES)
            ob_ref[0, rows, sl] = ((numer / sum(wts)) * _silu(zb[:, sl])).astype(ob_ref.dtype)
        return carry

    lax.fori_loop(0, rt // blk, merge, 0)


def _attn_prompt(qbs, zb, *, dils, blk, rt):
    B, L, _ = zb.shape
    args, specs = [], []
    for qb, d in zip(qbs, dils):
        qn, sb, npiece = _swa_geometry(rt, d, blk)
        for c in range(3):
            specs.append(pl.BlockSpec((1, d, qn, SW_WIDTH), lambda b, t, c=c: (b, 0, t, c)))
            args.append(qb)
        for c in (1, 2):
            for pi in range(npiece):
                back, per = npiece - pi, qn // sb
                specs.append(pl.BlockSpec(
                    (1, d, sb, SW_WIDTH),
                    lambda b, t, c=c, back=back, per=per: (b, 0, jnp.maximum(t * per - back, 0), c)))
                args.append(qb)
    specs.append(pl.BlockSpec((1, rt, SW_WIDTH), lambda b, t: (b, t, 0)))
    args.append(zb)
    npair = SW_WIDTH // LANES
    return pl.pallas_call(
        functools.partial(_attn_kernel, dils=dils, rt=rt, blk=blk),
        grid=(B, L // rt),
        in_specs=specs,
        out_specs=pl.BlockSpec((1, rt, SW_WIDTH), lambda b, t: (b, t, 0)),
        out_shape=jax.ShapeDtypeStruct((B, L, SW_WIDTH), BF16),
        scratch_shapes=[pltpu.VMEM((len(dils), npair, rt, LANES), F32),
                        pltpu.VMEM((len(dils), npair, rt, LANES), F32)],
        compiler_params=pltpu.CompilerParams(
            dimension_semantics=("arbitrary", "arbitrary"), vmem_limit_bytes=VMEM_LIMIT),
        name="attn_prompt",
    )(*args)


def _shift_window(src_ref, u, new_rows, new_scr, dst_ref):
    T = new_rows.shape[0]
    ncol = src_ref.shape[-1] // LANES
    shift = LANES - T
    new_scr[0:T, :] = new_rows
    new_t = new_scr[...].T
    lane = lax.broadcasted_iota(jnp.int32, (src_ref.shape[1], LANES), 1)
    cur = pltpu.roll(src_ref[u, :, 0:LANES], shift, 1)
    for j in range(ncol):
        if j + 1 < ncol:
            nxt = pltpu.roll(src_ref[u, :, (j + 1) * LANES:(j + 2) * LANES], shift, 1)
        else:
            nxt = pltpu.roll(new_t, shift, 1)
        dst_ref[u, :, j * LANES:(j + 1) * LANES] = jnp.where(lane < shift, cur, nxt)
        cur = nxt


def _sample_attention(qkv, zb, kt_ref, vt_ref, u, knew, vnew, ob_ref, dil):
    T = qkv.shape[0]
    W = kt_ref.shape[-1]
    R = T * SW_HEADS
    ncol = W // LANES

    head_of_lane = lax.broadcasted_iota(jnp.int32, (SW_HEADS, SW_WIDTH), 1) >> (SW_HD.bit_length() - 1)
    head_mask = head_of_lane == lax.broadcasted_iota(jnp.int32, (SW_HEADS, SW_WIDTH), 0)
    qbd = jnp.concatenate(
        [jnp.where(head_mask, jnp.broadcast_to(qkv[t:t + 1, 0:SW_WIDTH], (SW_HEADS, SW_WIDTH)), 0.0)
         for t in range(T)], axis=0).astype(BF16)
    hm_rows = jnp.concatenate([head_mask] * T, axis=0)

    scale = SW_HD ** -0.5

    s_cols = [_dot(qbd, kt_ref[u, :, j * LANES:(j + 1) * LANES].astype(BF16)) * scale for j in range(ncol)]
    s_cols.append(_dot_nt(qbd, knew.astype(BF16)) * scale)
    s_all = jnp.concatenate(s_cols, axis=1)

    wtot = W + LANES
    j_idx = lax.broadcasted_iota(jnp.int32, (R, wtot), 1)
    t_idx = lax.broadcasted_iota(jnp.int32, (R, wtot), 0) >> (SW_HEADS.bit_length() - 1)
    dist = jnp.where(j_idx < W, W + t_idx - j_idx, t_idx - (j_idx - W))
    exists = (j_idx < W + T) & (dist >= 0)

    ms, ls, ps = [], [], []
    for win, d in dil:
        valid = exists & ((dist & (d - 1)) == 0) & (dist <= win)
        sb = jnp.where(valid, s_all, -jnp.inf)
        m = jnp.max(sb, axis=-1, keepdims=True)
        pe = jnp.exp(sb - m)
        ms.append(m)
        ls.append(jnp.sum(pe, axis=-1, keepdims=True))
        ps.append(pe.astype(BF16))
    p_all = jnp.concatenate(ps, axis=0)
    num = _dot(p_all[:, W:], vnew.astype(BF16))
    for j in range(ncol):
        cols = slice(j * LANES, (j + 1) * LANES)
        num = num + _dot_nt(p_all[:, cols], vt_ref[u, :, cols].astype(BF16))
    mx = functools.reduce(jnp.maximum, ms)
    wts = [jnp.exp(m - mx) for m in ms]
    numer = sum(wts[i] * num[i * R:(i + 1) * R] for i in range(len(dil)))
    denom = sum(wts[i] * ls[i] for i in range(len(dil)))
    o = jnp.where(hm_rows, numer / denom, 0.0)
    gate = _silu(zb)
    for t in range(T):
        row = jnp.sum(o[t * SW_HEADS:(t + 1) * SW_HEADS], axis=0, keepdims=True)
        ob_ref[u, t:t + 1, :] = row * gate[t:t + 1]


def _out_kernel(x_ref, oa_ref, ob_ref, w_ref, fg_ref, y_ref):
    acc = _dot(oa_ref[0], w_ref[0:DN_WIDTH, :]) + _dot(ob_ref[0], w_ref[DN_WIDTH:, :])
    y = x_ref[0] + acc
    y_ref[0] = (y * lax.rsqrt(jnp.mean(y * y, axis=-1, keepdims=True) + NORM_EPS)) * fg_ref[...]


def _out(x, oa, ob, w, fg, *, tm, name):
    B, L, _ = x.shape
    row = lambda b, t: (b, t, 0)
    return pl.pallas_call(
        _out_kernel,
        grid=(B, L // tm),
        in_specs=[pl.BlockSpec((1, tm, D_MODEL), row), pl.BlockSpec((1, tm, DN_WIDTH), row),
                  pl.BlockSpec((1, tm, SW_WIDTH), row),
                  _const_spec((DN_WIDTH + SW_WIDTH, D_MODEL)), _const_spec((1, D_MODEL))],
        out_specs=pl.BlockSpec((1, tm, D_MODEL), row),
        out_shape=jax.ShapeDtypeStruct((B, L, D_MODEL), F32),
        compiler_params=pltpu.CompilerParams(
            dimension_semantics=("arbitrary", "arbitrary"), vmem_limit_bytes=VMEM_LIMIT),
        name=name,
    )(x, oa, ob, w, fg)


def _pad_lanes(v):
    return jnp.pad(v.astype(F32), (0, LANES - v.shape[0])).reshape(1, LANES)


def _tile(n, pref):
    return pref if n % pref == 0 else n


def _layers(xp, xs, conv_state, s0, k_win, v_win, wp, g_row, conv_w, alog_row, dtb_row, ng_row,
            w_out16, fg_row):
    B, T, _ = xs.shape
    W = k_win.shape[1]
    hist = CONV_W - 1
    group = SUBLANES
    assert T + hist <= group
    rows = xs.reshape(1, B * T, D_MODEL)
    raw, za, zb, ab, qkvb = _proj_sample(rows, g_row, wp, tm=_tile(B * T, 256))
    raw = raw.reshape(B, T, CONV_CH)
    qkvb = qkvb.reshape(B, T, 3 * SW_WIDTH)
    front = lambda a: jnp.pad(a.reshape(B, T, -1), ((0, 0), (group - T, 0), (0, 0))).reshape(1, B * group, -1)
    ext = jnp.concatenate([jnp.zeros((B, group - T - hist, CONV_CH), F32), conv_state, raw], axis=1)
    qkvn, gb = _gdn_pre(ext.reshape(1, B * group, CONV_CH), front(ab), conv_w, alog_row, dtb_row,
                        group=group, t_new=T)
    kt = jnp.transpose(k_win, (0, 2, 3, 1)).reshape(B, SW_WIDTH, W)
    vt = jnp.transpose(v_win, (0, 2, 3, 1)).reshape(B, SW_WIDTH, W)

    Bp, L, _ = xp.shape
    keep = min(WIN_MAX, L)
    dils = tuple(d for _, d in DILATIONS)
    tm = _tile(L, 512)
    p_qkvn, p_gb, p_za, p_zb, raw_last, kv_last, qb1, qb4, qb16 = _proj_prompt(
        xp, g_row, wp, conv_w, alog_row, dtb_row, dils=dils, keep=keep, tm=tm)
    p_oa, p_s, kt_new, vt_new, ob = _gdn_prompt(p_qkvn, p_gb, p_za, ng_row, kt, vt, qkvb,
                                                zb.reshape(B, T, SW_WIDTH), C=DN_CHUNK, TL=_tile(L, GDN_TILE))
    steps = {win // d for win, d in DILATIONS}
    assert len(steps) == 1, "every dilation looks back the same number of class rows"
    p_ob = _attn_prompt((qb1, qb4, qb16), p_zb, dils=dils, blk=steps.pop(), rt=_tile(L, 1024))
    yp = _out(xp, p_oa, p_ob, w_out16, fg_row, tm=tm, name="out_prompt")
    prompt = (yp, raw_last[:, HALO - hist:, :], p_s,
              kv_last[:, :, :SW_WIDTH].reshape(Bp, keep, SW_HEADS, SW_HD),
              kv_last[:, :, SW_WIDTH:].reshape(Bp, keep, SW_HEADS, SW_HD))

    oa, s_new = _gdn_sample(qkvn, gb, front(za), s0, ng_row, C=group, n=_tile(B * group, GDN_TILE))
    oa = oa.reshape(B, group, DN_WIDTH)[:, group - T:].reshape(1, B * T, DN_WIDTH)
    ys = _out(rows, oa, ob.reshape(1, B * T, SW_WIDTH).astype(BF16), w_out16, fg_row,
              tm=_tile(B * T, 256), name="out_sample")
    sample = (ys.reshape(B, T, D_MODEL), jnp.concatenate([conv_state, raw], axis=1)[:, -hist:], s_new,
              jnp.transpose(kt_new.reshape(B, SW_HEADS, SW_HD, W), (0, 3, 1, 2)),
              jnp.transpose(vt_new.reshape(B, SW_HEADS, SW_HD, W), (0, 3, 1, 2)))
    return prompt, sample


def kernel(x_prompt, x_sample, state_conv, state_delta, cache_k_win, cache_v_win, norm_g, final_norm_g,
           w_in, conv_w, a_log, dt_bias, dn_norm_g, w_out):
    assert norm_g.shape[0] == 1, "single layer"
    w = w_in[0]
    o1 = CONV_CH + DN_WIDTH
    o3 = o1 + 2 * DN_HEADS
    o4 = o3 + 3 * SW_WIDTH
    wp = jnp.concatenate(
        [w[:, :o1], w[:, o3:], w[:, o1:o3], jnp.zeros((D_MODEL, LANES - 2 * DN_HEADS), w.dtype)],
        axis=1).astype(BF16)
    assert wp.shape[1] == W_COLS and o4 - o3 == 3 * SW_WIDTH
    g_row = norm_g[0].reshape(1, D_MODEL)
    fg_row = final_norm_g.reshape(1, D_MODEL)
    alog_row = _pad_lanes(a_log[0])
    dtb_row = _pad_lanes(dt_bias[0])
    ng_row = dn_norm_g[0].reshape(1, DN_DV)
    w_out16 = w_out[0].astype(BF16)
    cw = conv_w[0]
    common = (wp, g_row, cw, alog_row, dtb_row, ng_row, w_out16, fg_row)
    (yp, pc, ps, pk, pv), (ys, sc, ss, sk, sv) = _layers(
        x_prompt, x_sample, state_conv[0], state_delta[0], cache_k_win[0], cache_v_win[0], *common)
    return (yp, ys, pc[None], ps[None], pk[None], pv[None], sc[None], ss[None], sk[None], sv[None])
```

```python
import functools

import numpy as np
import jax
import jax.numpy as jnp
from jax import lax
from jax.experimental import pallas as pl
from jax.experimental.pallas import tpu as pltpu

F32 = jnp.float32
BF16 = jnp.bfloat16

D_MODEL = 1024
DN_HEADS = 4
DN_DK = 128
DN_DV = 128
QK_W = DN_HEADS * DN_DK
DN_WIDTH = DN_HEADS * DN_DV
CONV_W = 4
CONV_CH = 2 * QK_W + DN_WIDTH
DN_CHUNK = 64
SW_HEADS = 8
SW_HD = 64
SW_WIDTH = SW_HEADS * SW_HD
DILATIONS = ((128, 1), (512, 4), (2048, 16))
WIN_MAX = 2048
NORM_EPS = 1e-6

LANES = 128
SUBLANES = 8
VMEM_LIMIT = 56 * 1024 * 1024

C_RAW = 0
C_ZA = CONV_CH
C_QKVB = C_ZA + DN_WIDTH
C_ZB = C_QKVB + 3 * SW_WIDTH
C_AB = C_ZB + SW_WIDTH
W_COLS = C_AB + LANES
HALO = SUBLANES
GDN_TILE = DN_HEADS * DN_CHUNK


def _dot(a, b):
    return lax.dot_general(a, b, (((a.ndim - 1,), (0,)), ((), ())), preferred_element_type=F32)


def _dot_nt(a, b):
    return lax.dot_general(a, b, (((1,), (1,)), ((), ())), preferred_element_type=F32)


def _dot_tn(a, b):
    return lax.dot_general(a, b, (((0,), (0,)), ((), ())), preferred_element_type=F32)


def _bdot(a, b):
    return lax.dot_general(a, b, (((2,), (1,)), ((0,), (0,))), preferred_element_type=F32)


def _bdot_tn(a, b):
    return lax.dot_general(a, b, (((1,), (1,)), ((0,), (0,))), preferred_element_type=F32)


def _dot_exact_lhs(a16, x):
    hi = x.astype(BF16)
    r1 = x - hi.astype(F32)
    mid = r1.astype(BF16)
    lo = (r1 - mid.astype(F32)).astype(BF16)
    return _dot(a16, lo) + _dot(a16, mid) + _dot(a16, hi)


def _sigmoid(x):
    return 1.0 / (1.0 + jnp.exp(-x))


def _silu(x):
    return x * _sigmoid(x)


def _conv_norm_gates(ext_scr, cw_ref, alog_ref, dtb_ref, qkvn_ref, gb_ref, rows, live_from):
    ch = LANES if rows % LANES == 0 else rows
    cw = cw_ref[...]
    a_neg = -jnp.exp(alog_ref[...])
    dtb = dtb_ref[...]
    lane = lax.broadcasted_iota(jnp.int32, (ch, LANES), 1)

    def chunk(c, carry):
        r0 = c * ch
        out = pl.ds(r0, ch)
        for s in range(CONV_CH // LANES):
            sl = slice(s * LANES, (s + 1) * LANES)
            win = ext_scr[pl.ds(r0, ch + HALO), sl]
            y = cw[0:1, sl] * win[HALO - 3:HALO - 3 + ch]
            y = y + cw[1:2, sl] * win[HALO - 2:HALO - 2 + ch]
            y = y + cw[2:3, sl] * win[HALO - 1:HALO - 1 + ch]
            y = y + cw[3:4, sl] * win[HALO:HALO + ch]
            v = _silu(y)
            if s < DN_HEADS:
                v = v * lax.rsqrt(jnp.sum(v * v, -1, keepdims=True) + NORM_EPS) * (DN_DK ** -0.5)
            elif s < 2 * DN_HEADS:
                v = v * lax.rsqrt(jnp.sum(v * v, -1, keepdims=True) + NORM_EPS)
            qkvn_ref[0, out, sl] = v.astype(qkvn_ref.dtype)
        ab = gb_ref[0, out, :]
        xg = ab + dtb
        softplus = jnp.maximum(xg, 0.0) + jnp.log1p(jnp.exp(-jnp.abs(xg)))
        gb = jnp.where(lane < DN_HEADS, a_neg * softplus, _sigmoid(ab))
        if live_from is not None:
            group, first = live_from
            rowi = lax.broadcasted_iota(jnp.int32, (ch, LANES), 0) + r0
            gb = jnp.where((rowi & (group - 1)) >= first, gb, 0.0)
        gb_ref[0, out, :] = gb
        return carry

    for c in range(rows // ch):
        chunk(c, 0)


def _normed(x_ref, g_ref, h_scr):
    rows = x_ref.shape[1]
    ch = LANES if rows % LANES == 0 else rows
    g = g_ref[...]

    for c in range(rows // ch):
        rws = pl.ds(c * ch, ch)
        x = x_ref[0, rws, :]
        ms = jnp.mean(x * x, axis=-1, keepdims=True)
        h_scr[rws, :] = ((x * lax.rsqrt(ms + NORM_EPS)) * g).astype(BF16)
    return h_scr[...]


def _proj_sample_kernel(x_ref, g_ref, w_ref, raw_ref, za_ref, zb_ref, ab_ref, qb_ref, h_scr):
    h = _normed(x_ref, g_ref, h_scr)
    raw_ref[0] = _dot(h, w_ref[:, C_RAW:C_ZA])
    za_ref[0] = _dot(h, w_ref[:, C_ZA:C_QKVB])
    zb_ref[0] = _dot(h, w_ref[:, C_ZB:C_AB])
    ab_ref[0] = _dot(h, w_ref[:, C_AB:W_COLS])
    qb_ref[0] = _dot(h, w_ref[:, C_QKVB:C_ZB])


def _proj_prompt_kernel(x_ref, g_ref, w_ref, cw_ref, alog_ref, dtb_ref,
                        qkvn_ref, gb_ref, za_ref, zb_ref, rawlast_ref, kv_ref, *rest, dils, tm):
    nd = len(dils)
    qb_refs = rest[:nd]
    cls_scrs = list(rest[nd:2 * nd - 1]) + [None]
    ext_scr, h_scr = rest[2 * nd - 1:]

    @pl.when(pl.program_id(1) == 0)
    def _():
        ext_scr[0:HALO, :] = jnp.zeros((HALO, CONV_CH), F32)

    _normed(x_ref, g_ref, h_scr)
    qkvb = _dot(h_scr[...], w_ref[:, C_QKVB:C_ZB])
    ext_scr[HALO:HALO + tm, :] = _dot(h_scr[...], w_ref[:, C_RAW:C_ZA])
    za_ref[0] = _dot(h_scr[...], w_ref[:, C_ZA:C_QKVB]).astype(za_ref.dtype)
    zb_ref[0] = _dot(h_scr[...], w_ref[:, C_ZB:C_AB]).astype(zb_ref.dtype)
    gb_ref[0] = _dot(h_scr[...], w_ref[:, C_AB:W_COLS])

    kv_ref[0] = qkvb[:, SW_WIDTH:]
    nslab = 3 * SW_WIDTH // LANES
    assert dils[0] == 1
    qb_refs[0][0, 0] = qkvb.astype(BF16)
    for j in range(nslab):
        cls_scrs[0][j] = qkvb[:, j * LANES:(j + 1) * LANES]
    for i in range(1, nd):
        d, prev = dils[i], dils[i - 1]
        ratio, n_d, n_prev = d // prev, tm // d, tm // prev
        assert d % prev == 0
        for r in range(d):
            for j in range(nslab):
                v = cls_scrs[i - 1][j, pl.ds((r % prev) * n_prev + r // prev, n_d, stride=ratio), :]
                qb_refs[i][0, r, :, j * LANES:(j + 1) * LANES] = v.astype(BF16)
                if cls_scrs[i] is not None:
                    cls_scrs[i][j, r * n_d:(r + 1) * n_d, :] = v

    _conv_norm_gates(ext_scr, cw_ref, alog_ref, dtb_ref, qkvn_ref, gb_ref, tm, None)
    last = ext_scr[tm:tm + HALO, :]
    rawlast_ref[0] = last
    ext_scr[0:HALO, :] = last


def _const_spec(shape):
    zeros = (0,) * len(shape)
    return pl.BlockSpec(shape, lambda *_: zeros)


def _proj_prompt(x, g, w, cw, alog_row, dtb_row, *, dils, keep, tm):
    B, L, _ = x.shape
    row = lambda b, t: (b, t, 0)
    first = (L - keep) // tm
    out_shape = [jax.ShapeDtypeStruct((B, L, CONV_CH), F32),
                 jax.ShapeDtypeStruct((B, L, LANES), F32),
                 jax.ShapeDtypeStruct((B, L, DN_WIDTH), BF16),
                 jax.ShapeDtypeStruct((B, L, SW_WIDTH), BF16),
                 jax.ShapeDtypeStruct((B, HALO, CONV_CH), F32),
                 jax.ShapeDtypeStruct((B, keep, 2 * SW_WIDTH), F32)]
    out_specs = [pl.BlockSpec((1, tm, CONV_CH), row),
                 pl.BlockSpec((1, tm, LANES), row),
                 pl.BlockSpec((1, tm, DN_WIDTH), row),
                 pl.BlockSpec((1, tm, SW_WIDTH), row),
                 pl.BlockSpec((1, HALO, CONV_CH), lambda b, t: (b, 0, 0)),
                 pl.BlockSpec((1, tm, 2 * SW_WIDTH), lambda b, t: (b, jnp.maximum(t - first, 0), 0))]
    for d in dils:
        out_shape.append(jax.ShapeDtypeStruct((B, d, L // d, 3 * SW_WIDTH), BF16))
        out_specs.append(pl.BlockSpec((1, d, tm // d, 3 * SW_WIDTH), lambda b, t: (b, 0, t, 0)))
    return pl.pallas_call(
        functools.partial(_proj_prompt_kernel, dils=dils, tm=tm),
        grid=(B, L // tm),
        in_specs=[pl.BlockSpec((1, tm, D_MODEL), row),
                  _const_spec((1, D_MODEL)),
                  pl.BlockSpec((D_MODEL, W_COLS), lambda b, t: (0, 0), pipeline_mode=pl.Buffered(1)),
                  _const_spec((CONV_W, CONV_CH)),
                  _const_spec((1, LANES)),
                  _const_spec((1, LANES))],
        out_specs=out_specs,
        out_shape=out_shape,
        scratch_shapes=[pltpu.VMEM((3 * SW_WIDTH // LANES, tm, LANES), F32)] * (len(dils) - 1)
        + [pltpu.VMEM((tm + HALO, CONV_CH), F32), pltpu.VMEM((tm, D_MODEL), BF16)],
        compiler_params=pltpu.CompilerParams(
            dimension_semantics=("arbitrary", "arbitrary"), vmem_limit_bytes=VMEM_LIMIT),
        name="proj_prompt",
    )(x, g, w, cw, alog_row, dtb_row)


def _proj_sample(x, g, w, *, tm):
    B, L, _ = x.shape
    row = lambda b, t: (b, t, 0)
    widths = (CONV_CH, DN_WIDTH, SW_WIDTH, LANES, 3 * SW_WIDTH)
    return pl.pallas_call(
        _proj_sample_kernel,
        grid=(B, L // tm),
        in_specs=[pl.BlockSpec((1, tm, D_MODEL), row), _const_spec((1, D_MODEL)),
                  _const_spec((D_MODEL, W_COLS))],
        out_specs=[pl.BlockSpec((1, tm, c), row) for c in widths],
        out_shape=[jax.ShapeDtypeStruct((B, L, c), F32) for c in widths],
        scratch_shapes=[pltpu.VMEM((tm, D_MODEL), BF16)],
        compiler_params=pltpu.CompilerParams(
            dimension_semantics=("arbitrary", "arbitrary"), vmem_limit_bytes=VMEM_LIMIT),
        name="proj_sample",
    )(x, g, w)


def _gdn_pre_kernel(ext_ref, ab_ref, cw_ref, alog_ref, dtb_ref, qkvn_ref, gb_ref, ext_scr, *, rows, group, t_new):
    ext_scr[0:HALO, :] = jnp.zeros((HALO, CONV_CH), F32)
    ext_scr[HALO:HALO + rows, :] = ext_ref[0]
    gb_ref[0] = ab_ref[0]
    _conv_norm_gates(ext_scr, cw_ref, alog_ref, dtb_ref, qkvn_ref, gb_ref, rows, (group, group - t_new))


def _gdn_pre(ext, ab, cw, alog_row, dtb_row, *, group, t_new):
    _, rows, _ = ext.shape
    return pl.pallas_call(
        functools.partial(_gdn_pre_kernel, rows=rows, group=group, t_new=t_new),
        grid=(1,),
        in_specs=[_const_spec((1, rows, CONV_CH)), _const_spec((1, rows, LANES)),
                  _const_spec((CONV_W, CONV_CH)), _const_spec((1, LANES)), _const_spec((1, LANES))],
        out_specs=[_const_spec((1, rows, CONV_CH)), _const_spec((1, rows, LANES))],
        out_shape=[jax.ShapeDtypeStruct((1, rows, CONV_CH), F32), jax.ShapeDtypeStruct((1, rows, LANES), F32)],
        scratch_shapes=[pltpu.VMEM((rows + HALO, CONV_CH), F32)],
        compiler_params=pltpu.CompilerParams(
            dimension_semantics=("arbitrary",), vmem_limit_bytes=VMEM_LIMIT),
        name="gdn_pre_sample",
    )(ext, ab, cw, alog_row, dtb_row)


def _block_masks(n, c):
    r = np.arange(n)[:, None]
    cc = np.arange(n)[None, :]
    x = r ^ cc
    same = x < c
    incl = same & (cc <= r)
    levels = []
    k = 1
    while (2 << k) <= c:
        levels.append((x >> k) == 1)
        k += 1
    masks = np.stack([r == cc, incl, (x >> 1) == 0]).astype(np.float32)
    lmask = np.concatenate([incl, same] + levels, axis=0).astype(np.float32)
    return jnp.asarray(masks), jnp.asarray(lmask, dtype=BF16)


def _gdn_tiles(tiles, masks_ref, lmask_ref):
    n = tiles[0][0].shape[0]
    nt = len(tiles)
    nlev = lmask_ref.shape[0] // n - 2
    eye = masks_ref[0]
    incl = masks_ref[1]
    g_b = [jnp.broadcast_to(t[3], (n, LANES)) for t in tiles]
    beta_b = [jnp.broadcast_to(t[4], (n, LANES)) for t in tiles]
    cs = _dot_exact_lhs(lmask_ref[0:2 * n, :], jnp.concatenate(g_b, axis=1))
    gc_b = [cs[:n, i * LANES:(i + 1) * LANES] for i in range(nt)]
    gl_b = [cs[n:, i * LANES:(i + 1) * LANES] for i in range(nt)]
    kb = [tiles[i][1] * beta_b[i] for i in range(nt)]
    k16 = [t[1].astype(BF16) for t in tiles]
    kk = [_dot_nt(kb[i].astype(BF16), k16[i]) for i in range(nt)]
    qk = [_dot_nt(tiles[i][0].astype(BF16), k16[i]) for i in range(nt)]
    a16, attn, t = [], [], []
    for i in range(nt):
        gc_sq = jnp.concatenate([gc_b[i]] * (n // LANES), axis=1)
        gc_row = jnp.sum(gc_sq * eye, axis=0, keepdims=True)
        dec = jnp.exp(jnp.minimum(gc_sq - gc_row, 0.0)) * incl
        a = kk[i] * (dec - eye)
        t.append(eye - a * masks_ref[2])
        a16.append(a.astype(BF16))
        attn.append((qk[i] * dec).astype(BF16))
    for lv in range(nlev):
        t16 = [x.astype(BF16) for x in t]
        m16 = lmask_ref[(2 + lv) * n:(3 + lv) * n, :]
        p = [_dot(t16[i], a16[i] * m16) for i in range(nt)]
        t = [t[i] - _dot(p[i].astype(BF16), t16[i]) for i in range(nt)]
    e_gc = [jnp.exp(x) for x in gc_b]
    sol = [_dot(t[i].astype(BF16),
                jnp.concatenate([tiles[i][2] * beta_b[i], kb[i] * e_gc[i]], axis=1).astype(BF16))
           for i in range(nt)]
    return [(sol[i][:, :DN_DV], sol[i][:, DN_DV:], tiles[i][0] * e_gc[i],
             tiles[i][1] * jnp.exp(gl_b[i] - gc_b[i]), attn[i], gl_b[i]) for i in range(nt)]


def _out_norm(o, ng, za):
    return ((o * lax.rsqrt(jnp.mean(o * o, -1, keepdims=True) + NORM_EPS)) * ng) * _silu(za.astype(F32))


def _gdn_prompt_kernel(qkvn_ref, gb_ref, za_ref, ng_ref, masks_ref, lmask_ref, kt_ref, vt_ref, new_ref,
                       zbs_ref, o_ref, s_out_ref, kto_ref, vto_ref, obs_ref, s_scr, knew_scr, vnew_scr,
                       *, C, TL, G):
    H = DN_HEADS

    @pl.when(pl.program_id(1) == 0)
    def _():
        s_scr[...] = jnp.zeros_like(s_scr)

    @pl.when((pl.program_id(0) == 0) & (pl.program_id(1) == 0))
    def _():
        knew_scr[...] = jnp.zeros_like(knew_scr)
        vnew_scr[...] = jnp.zeros_like(vnew_scr)

    for u in range(kt_ref.shape[0]):
        _shift_window(kt_ref, u, new_ref[u, :, SW_WIDTH:2 * SW_WIDTH], knew_scr, kto_ref)
        _shift_window(vt_ref, u, new_ref[u, :, 2 * SW_WIDTH:], vnew_scr, vto_ref)
        _sample_attention(new_ref[u], zbs_ref[u], kt_ref, vt_ref, u, knew_scr[...], vnew_scr[...],
                          obs_ref, DILATIONS)

    ng = ng_ref[...]

    def group(gi, carry):
        base = gi * (G * C) if isinstance(gi, int) else pl.multiple_of(gi * (G * C), G * C)
        rows = [pl.ds(base + i * C, C) for i in range(G)]

        def stack(ref, rws, off):
            return jnp.concatenate(
                [ref[0, rws, off + h * LANES:off + (h + 1) * LANES] for h in range(H)], axis=0)

        tiles, zas = [], []
        for rws in rows:
            gbc = gb_ref[0, rws, :]
            g_col = jnp.concatenate([gbc[:, h:h + 1] for h in range(H)], axis=0)
            beta_col = jnp.concatenate([gbc[:, H + h:H + h + 1] for h in range(H)], axis=0)
            zas.append(stack(za_ref, rws, 0))
            tiles.append((stack(qkvn_ref, rws, 0).astype(F32), stack(qkvn_ref, rws, QK_W).astype(F32),
                          stack(qkvn_ref, rws, 2 * QK_W).astype(F32), g_col, beta_col))
        tiles = _gdn_tiles(tiles, masks_ref, lmask_ref)
        s = [s_scr[h] for h in range(H)]
        outs = []
        for (u, w, q_dec, k_dec, attn, gl_b), za in zip(tiles, zas):
            ws, qs = [], []
            for h in range(H):
                blk = slice(h * C, (h + 1) * C)
                r = _dot(jnp.concatenate([w[blk], q_dec[blk]], axis=0).astype(BF16), s[h].astype(BF16))
                ws.append(r[:C])
                qs.append(r[C:])
            v_new = u - jnp.concatenate(ws, axis=0)
            vn16 = v_new.astype(BF16)
            o = jnp.concatenate(qs, axis=0) + _dot(attn.astype(BF16), vn16)
            for h in range(H):
                blk = slice(h * C, (h + 1) * C)
                gl_s = jnp.broadcast_to(jnp.exp(gl_b[h * C:h * C + 1, :]), (DN_DK, DN_DV))
                s[h] = s[h] * gl_s + _dot_tn(k_dec[blk].astype(BF16), vn16[blk])
            outs.append(_out_norm(o, ng, za))
        for rws, out in zip(rows, outs):
            for h in range(H):
                o_ref[0, rws, h * DN_DV:(h + 1) * DN_DV] = out[h * C:(h + 1) * C].astype(o_ref.dtype)
        for h in range(H):
            s_scr[h] = s[h]
        return carry

    for gi in range(TL // (G * C)):
        group(gi, 0)
    s_out_ref[0] = s_scr[...]


def _gdn_prompt(qkvn, gb, za, ng_row, kt, vt, new_rows, zbs, *, C, TL):
    B, L, _ = qkvn.shape
    n = DN_HEADS * C
    nt = L // TL
    Bs, T, _ = new_rows.shape
    W = kt.shape[-1]
    assert Bs % (B * nt) == 0
    per = Bs // (B * nt)
    masks, lmask = _block_masks(n, C)
    row = lambda b, t: (b, t, 0)
    share = lambda b, t: (b * nt + t, 0, 0)
    window = pl.BlockSpec((per, SW_WIDTH, W), share)
    return pl.pallas_call(
        functools.partial(_gdn_prompt_kernel, C=C, TL=TL, G=4 if TL % (4 * C) == 0 else 1),
        grid=(B, nt),
        in_specs=[pl.BlockSpec((1, TL, CONV_CH), row),
                  pl.BlockSpec((1, TL, LANES), row),
                  pl.BlockSpec((1, TL, DN_WIDTH), row),
                  _const_spec((1, DN_DV)), _const_spec(masks.shape), _const_spec(lmask.shape),
                  window, window, pl.BlockSpec((per, T, 3 * SW_WIDTH), share),
                  pl.BlockSpec((per, T, SW_WIDTH), share)],
        out_specs=[pl.BlockSpec((1, TL, DN_WIDTH), row),
                   pl.BlockSpec((1, DN_HEADS, DN_DK, DN_DV), lambda b, t: (b, 0, 0, 0)),
                   window, window, pl.BlockSpec((per, T, SW_WIDTH), share)],
        out_shape=[jax.ShapeDtypeStruct((B, L, DN_WIDTH), BF16),
                   jax.ShapeDtypeStruct((B, DN_HEADS, DN_DK, DN_DV), F32),
                   jax.ShapeDtypeStruct(kt.shape, F32), jax.ShapeDtypeStruct(vt.shape, F32),
                   jax.ShapeDtypeStruct((Bs, T, SW_WIDTH), F32)],
        scratch_shapes=[pltpu.VMEM((DN_HEADS, DN_DK, DN_DV), F32),
                        pltpu.VMEM((LANES, SW_WIDTH), F32), pltpu.VMEM((LANES, SW_WIDTH), F32)],
        compiler_params=pltpu.CompilerParams(
            dimension_semantics=("arbitrary", "arbitrary"), vmem_limit_bytes=VMEM_LIMIT),
        name="gdn_prompt",
    )(qkvn, gb, za, ng_row, masks, lmask, kt, vt, new_rows, zbs)


def _gdn_sample_kernel(qkvn_ref, gb_ref, za_ref, s0_ref, ng_ref, masks_ref, lmask_ref, o_ref, s_out_ref, *, C):
    n = qkvn_ref.shape[1]
    nb = n // C
    ng = ng_ref[...]
    split = lambda a: a.reshape(nb, C, a.shape[-1])
    tiles = _gdn_tiles(
        [(qkvn_ref[0, :, h * LANES:(h + 1) * LANES],
          qkvn_ref[0, :, QK_W + h * LANES:QK_W + (h + 1) * LANES],
          qkvn_ref[0, :, 2 * QK_W + h * LANES:2 * QK_W + (h + 1) * LANES],
          gb_ref[0, :, h:h + 1], gb_ref[0, :, DN_HEADS + h:DN_HEADS + h + 1]) for h in range(DN_HEADS)],
        masks_ref, lmask_ref)
    for h in range(DN_HEADS):
        sl = slice(h * LANES, (h + 1) * LANES)
        u, w, q_dec, k_dec, attn, gl_b = tiles[h]
        s0 = s0_ref[:, h]
        r = _bdot(jnp.concatenate([split(w), split(q_dec)], axis=1).astype(BF16), s0.astype(BF16))
        v_new = split(u) - r[:, :C]
        vn16 = v_new.astype(BF16)
        o = r[:, C:].reshape(n, DN_DV) + _dot(attn.astype(BF16), vn16.reshape(n, DN_DV))
        gl = split(jnp.exp(gl_b))[:, 0:1, :]
        s_out_ref[:, h] = s0 * gl + _bdot_tn(split(k_dec).astype(BF16), vn16)
        o_ref[0, :, sl] = _out_norm(o, ng, za_ref[0, :, sl]).astype(o_ref.dtype)


def _gdn_sample(qkvn, gb, za, s0, ng_row, *, C, n):
    _, rows, _ = qkvn.shape
    nb = n // C
    masks, lmask = _block_masks(n, C)
    row = lambda i: (0, i, 0)
    state = pl.BlockSpec((nb, DN_HEADS, DN_DK, DN_DV), lambda i: (i, 0, 0, 0))
    return pl.pallas_call(
        functools.partial(_gdn_sample_kernel, C=C),
        grid=(rows // n,),
        in_specs=[pl.BlockSpec((1, n, CONV_CH), row), pl.BlockSpec((1, n, LANES), row),
                  pl.BlockSpec((1, n, DN_WIDTH), row), state,
                  _const_spec((1, DN_DV)), _const_spec(masks.shape), _const_spec(lmask.shape)],
        out_specs=[pl.BlockSpec((1, n, DN_WIDTH), row), state],
        out_shape=[jax.ShapeDtypeStruct((1, rows, DN_WIDTH), BF16), jax.ShapeDtypeStruct(s0.shape, F32)],
        compiler_params=pltpu.CompilerParams(
            dimension_semantics=("arbitrary",), vmem_limit_bytes=VMEM_LIMIT),
        name="gdn_sample",
    )(qkvn, gb, za, s0, ng_row, masks, lmask)


def _swa_blocks(items):
    mq = items[0][0].shape[0]
    npair = SW_WIDTH // LANES
    nh = 2 * npair
    lo_half = lax.broadcasted_iota(jnp.int32, (mq, LANES), 1) < SW_HD
    scale = SW_HD ** -0.5
    half_masks = (jnp.where(lo_half, scale, 0.0).astype(BF16), jnp.where(lo_half, 0.0, scale).astype(BF16))
    scores = []
    for q, kk, _, _ in items:
        for p in range(npair):
            sl = slice(p * LANES, (p + 1) * LANES)
            for half in range(2):
                scores.append(_dot_nt(q[:, sl] * half_masks[half], kk[:, sl]))
    probs, inv_l, lses = [], [], []
    for i, s in enumerate(scores):
        s = jnp.where(items[i // nh][3], s, -jnp.inf)
        m = jnp.max(s, axis=-1, keepdims=True)
        pe = jnp.exp(s - m)
        l = jnp.sum(pe, axis=-1, keepdims=True)
        probs.append(pe.astype(BF16))
        inv_l.append(1.0 / l)
        lses.append(jnp.broadcast_to(m + jnp.log(l), (mq, LANES)))
    pvs = [_dot(probs[i], items[i // nh][2][:, ((i % nh) // 2) * LANES:((i % nh) // 2 + 1) * LANES])
           for i in range(len(scores))]
    out = []
    for b in range(len(items)):
        e = [b * nh + 2 * p for p in range(npair)]
        out.append([(jnp.where(lo_half, pvs[i] * inv_l[i], pvs[i + 1] * inv_l[i + 1]),
                     jnp.where(lo_half, lses[i], lses[i + 1])) for i in e])
    return out


def _swa_geometry(rt, d, blk):
    qn = rt // d
    sb = min(blk, qn)
    return qn, sb, blk // sb


def _attn_kernel(*refs, dils, rt, blk):
    nbr = len(dils)
    zb_ref, ob_ref, o_scr, lse_scr = refs[-4:]
    t = pl.program_id(1)
    npair = SW_WIDTH // LANES
    idx = 0
    for bi, d in enumerate(dils):
        qn, sb, npiece = _swa_geometry(rt, d, blk)
        q_ref, k_ref, v_ref = refs[idx:idx + 3]
        kp = refs[idx + 3:idx + 3 + npiece]
        vp = refs[idx + 3 + npiece:idx + 3 + 2 * npiece]
        idx += 3 + 2 * npiece
        nk = blk + sb
        qi = lax.broadcasted_iota(jnp.int32, (sb, nk), 0)
        kj = lax.broadcasted_iota(jnp.int32, (sb, nk), 1)
        band = (kj >= qi) & (kj <= qi + blk)
        band0 = band & (kj >= blk - t * qn)

        nsub = qn // sb

        def first(r, q_ref=q_ref, k_ref=k_ref, v_ref=v_ref, kp=kp, vp=vp, sb=sb, band0=band0):
            kk = jnp.concatenate([x[0, r] for x in kp] + [k_ref[0, r, 0:sb, :]], axis=0)
            vv = jnp.concatenate([x[0, r] for x in vp] + [v_ref[0, r, 0:sb, :]], axis=0)
            return r, 0, (q_ref[0, r, 0:sb, :], kk, vv, band0)

        def later(r, j, q_ref=q_ref, k_ref=k_ref, v_ref=v_ref, sb=sb, band=band):
            aligned = (lambda x: x) if isinstance(j, int) else (lambda x: pl.multiple_of(x, sb))
            keys = pl.ds(aligned((j - 1) * sb), 2 * sb)
            return r, j, (q_ref[0, r, pl.ds(aligned(j * sb), sb), :],
                          k_ref[0, r, keys, :], v_ref[0, r, keys, :], band)

        def run(blocks, bi=bi, d=d, sb=sb):
            for (r, j, _), res in zip(blocks, _swa_blocks([b[2] for b in blocks])):
                start = j * (sb * d) + r
                rows = pl.ds(start, sb, stride=d) if d > 1 else pl.ds(start, sb)
                for p in range(npair):
                    o_scr[bi, p, rows, :] = res[p][0]
                    lse_scr[bi, p, rows, :] = res[p][1]

        if nsub == 1:
            gs = 4
            assert d % gs == 0
            lax.fori_loop(0, d // gs, lambda g, c, run=run, first=first:
                          (run([first(gs * g + i) for i in range(gs)]), c)[1], 0)
        else:
            assert nsub % 2 == 0

            def do_class(r, run=run, first=first, later=later, nsub=nsub):
                run([first(r), later(r, 1)])
                if nsub > 2:
                    lax.fori_loop(1, nsub // 2, lambda g, c:
                                  (run([later(r, 2 * g), later(r, 2 * g + 1)]), c)[1], 0)

            if d == 1:
                do_class(0)
            else:
                lax.fori_loop(0, d, lambda r, c, f=do_class: (f(r), c)[1], 0)

    def merge(c, carry):
        rows = pl.ds(pl.multiple_of(c * blk, blk), blk)
        zb = zb_ref[0, rows, :].astype(F32)
        for p in range(npair):
            lses = [lse_scr[i, p, rows, :] for i in range(nbr)]
            mx = functools.reduce(jnp.maximum, lses)
            wts = [jnp.exp(l - mx) for l in lses]
            numer = sum(wts[i] * o_scr[i, p, rows, :] for i in range(nbr))
            sl = slice(p * LANES, (p + 1) * LANES)
            ob_ref[0, rows, sl] = ((numer / sum(wts)) * _silu(zb[:, sl])).astype(ob_ref.dtype)
        return carry

    lax.fori_loop(0, rt // blk, merge, 0)


def _attn_prompt(qbs, zb, *, dils, blk, rt):
    B, L, _ = zb.shape
    args, specs = [], []
    for qb, d in zip(qbs, dils):
        qn, sb, npiece = _swa_geometry(rt, d, blk)
        for c in range(3):
            specs.append(pl.BlockSpec((1, d, qn, SW_WIDTH), lambda b, t, c=c: (b, 0, t, c)))
            args.append(qb)
        for c in (1, 2):
            for pi in range(npiece):
                back, per = npiece - pi, qn // sb
                specs.append(pl.BlockSpec(
                    (1, d, sb, SW_WIDTH),
                    lambda b, t, c=c, back=back, per=per: (b, 0, jnp.maximum(t * per - back, 0), c)))
                args.append(qb)
    specs.append(pl.BlockSpec((1, rt, SW_WIDTH), lambda b, t: (b, t, 0)))
    args.append(zb)
    npair = SW_WIDTH // LANES
    return pl.pallas_call(
        functools.partial(_attn_kernel, dils=dils, rt=rt, blk=blk),
        grid=(B, L // rt),
        in_specs=specs,
        out_specs=pl.BlockSpec((1, rt, SW_WIDTH), lambda b, t: (b, t, 0)),
        out_shape=jax.ShapeDtypeStruct((B, L, SW_WIDTH), BF16),
        scratch_shapes=[pltpu.VMEM((len(dils), npair, rt, LANES), F32),
                        pltpu.VMEM((len(dils), npair, rt, LANES), F32)],
        compiler_params=pltpu.CompilerParams(
            dimension_semantics=("arbitrary", "arbitrary"), vmem_limit_bytes=VMEM_LIMIT),
        name="attn_prompt",
    )(*args)


def _shift_window(src_ref, u, new_rows, new_scr, dst_ref):
    T = new_rows.shape[0]
    ncol = src_ref.shape[-1] // LANES
    shift = LANES - T
    new_scr[0:T, :] = new_rows
    new_t = new_scr[...].T
    lane = lax.broadcasted_iota(jnp.int32, (src_ref.shape[1], LANES), 1)
    cur = pltpu.roll(src_ref[u, :, 0:LANES], shift, 1)
    for j in range(ncol):
        if j + 1 < ncol:
            nxt = pltpu.roll(src_ref[u, :, (j + 1) * LANES:(j + 2) * LANES], shift, 1)
        else:
            nxt = pltpu.roll(new_t, shift, 1)
        dst_ref[u, :, j * LANES:(j + 1) * LANES] = jnp.where(lane < shift, cur, nxt)
        cur = nxt


def _sample_attention(qkv, zb, kt_ref, vt_ref, u, knew, vnew, ob_ref, dil):
    T = qkv.shape[0]
    W = kt_ref.shape[-1]
    R = T * SW_HEADS
    ncol = W // LANES

    head_of_lane = lax.broadcasted_iota(jnp.int32, (SW_HEADS, SW_WIDTH), 1) >> (SW_HD.bit_length() - 1)
    head_mask = head_of_lane == lax.broadcasted_iota(jnp.int32, (SW_HEADS, SW_WIDTH), 0)
    qbd = jnp.concatenate(
        [jnp.where(head_mask, jnp.broadcast_to(qkv[t:t + 1, 0:SW_WIDTH], (SW_HEADS, SW_WIDTH)), 0.0)
         for t in range(T)], axis=0).astype(BF16)
    hm_rows = jnp.concatenate([head_mask] * T, axis=0)

    scale = SW_HD ** -0.5

    s_cols = [_dot(qbd, kt_ref[u, :, j * LANES:(j + 1) * LANES].astype(BF16)) * scale for j in range(ncol)]
    s_cols.append(_dot_nt(qbd, knew.astype(BF16)) * scale)
    s_all = jnp.concatenate(s_cols, axis=1)

    wtot = W + LANES
    j_idx = lax.broadcasted_iota(jnp.int32, (R, wtot), 1)
    t_idx = lax.broadcasted_iota(jnp.int32, (R, wtot), 0) >> (SW_HEADS.bit_length() - 1)
    dist = jnp.where(j_idx < W, W + t_idx - j_idx, t_idx - (j_idx - W))
    exists = (j_idx < W + T) & (dist >= 0)

    ms, ls, ps = [], [], []
    for win, d in dil:
        valid = exists & ((dist & (d - 1)) == 0) & (dist <= win)
        sb = jnp.where(valid, s_all, -jnp.inf)
        m = jnp.max(sb, axis=-1, keepdims=True)
        pe = jnp.exp(sb - m)
        ms.append(m)
        ls.append(jnp.sum(pe, axis=-1, keepdims=True))
        ps.append(pe.astype(BF16))
    p_all = jnp.concatenate(ps, axis=0)
    num = _dot(p_all[:, W:], vnew.astype(BF16))
    for j in range(ncol):
        cols = slice(j * LANES, (j + 1) * LANES)
        num = num + _dot_nt(p_all[:, cols], vt_ref[u, :, cols].astype(BF16))
    mx = functools.reduce(jnp.maximum, ms)
    wts = [jnp.exp(m - mx) for m in ms]
    numer = sum(wts[i] * num[i * R:(i + 1) * R] for i in range(len(dil)))
    denom = sum(wts[i] * ls[i] for i in range(len(dil)))
    o = jnp.where(hm_rows, numer / denom, 0.0)
    gate = _silu(zb)
    for t in range(T):
        row = jnp.sum(o[t * SW_HEADS:(t + 1) * SW_HEADS], axis=0, keepdims=True)
        ob_ref[u, t:t + 1, :] = row * gate[t:t + 1]


def _out_kernel(x_ref, oa_ref, ob_ref, w_ref, fg_ref, y_ref):
    acc = _dot(oa_ref[0], w_ref[0:DN_WIDTH, :]) + _dot(ob_ref[0], w_ref[DN_WIDTH:, :])
    y = x_ref[0] + acc
    y_ref[0] = (y * lax.rsqrt(jnp.mean(y * y, axis=-1, keepdims=True) + NORM_EPS)) * fg_ref[...]


def _out(x, oa, ob, w, fg, *, tm, name):
    B, L, _ = x.shape
    row = lambda b, t: (b, t, 0)
    return pl.pallas_call(
        _out_kernel,
        grid=(B, L // tm),
        in_specs=[pl.BlockSpec((1, tm, D_MODEL), row), pl.BlockSpec((1, tm, DN_WIDTH), row),
                  pl.BlockSpec((1, tm, SW_WIDTH), row),
                  _const_spec((DN_WIDTH + SW_WIDTH, D_MODEL)), _const_spec((1, D_MODEL))],
        out_specs=pl.BlockSpec((1, tm, D_MODEL), row),
        out_shape=jax.ShapeDtypeStruct((B, L, D_MODEL), F32),
        compiler_params=pltpu.CompilerParams(
            dimension_semantics=("arbitrary", "arbitrary"), vmem_limit_bytes=VMEM_LIMIT),
        name=name,
    )(x, oa, ob, w, fg)


def _pad_lanes(v):
    return jnp.pad(v.astype(F32), (0, LANES - v.shape[0])).reshape(1, LANES)


def _tile(n, pref):
    return pref if n % pref == 0 else n


def _layers(xp, xs, conv_state, s0, k_win, v_win, wp, g_row, conv_w, alog_row, dtb_row, ng_row,
            w_out16, fg_row):
    B, T, _ = xs.shape
    W = k_win.shape[1]
    hist = CONV_W - 1
    group = SUBLANES
    assert T + hist <= group
    rows = xs.reshape(1, B * T, D_MODEL)
    raw, za, zb, ab, qkvb = _proj_sample(rows, g_row, wp, tm=_tile(B * T, 256))
    raw = raw.reshape(B, T, CONV_CH)
    qkvb = qkvb.reshape(B, T, 3 * SW_WIDTH)
    front = lambda a: jnp.pad(a.reshape(B, T, -1), ((0, 0), (group - T, 0), (0, 0))).reshape(1, B * group, -1)
    ext = jnp.concatenate([jnp.zeros((B, group - T - hist, CONV_CH), F32), conv_state, raw], axis=1)
    qkvn, gb = _gdn_pre(ext.reshape(1, B * group, CONV_CH), front(ab), conv_w, alog_row, dtb_row,
                        group=group, t_new=T)
    kt = jnp.transpose(k_win, (0, 2, 3, 1)).reshape(B, SW_WIDTH, W)
    vt = jnp.transpose(v_win, (0, 2, 3, 1)).reshape(B, SW_WIDTH, W)

    Bp, L, _ = xp.shape
    keep = min(WIN_MAX, L)
    dils = tuple(d for _, d in DILATIONS)
    tm = _tile(L, 512)
    p_qkvn, p_gb, p_za, p_zb, raw_last, kv_last, qb1, qb4, qb16 = _proj_prompt(
        xp, g_row, wp, conv_w, alog_row, dtb_row, dils=dils, keep=keep, tm=tm)
    p_oa, p_s, kt_new, vt_new, ob = _gdn_prompt(p_qkvn, p_gb, p_za, ng_row, kt, vt, qkvb,
                                                zb.reshape(B, T, SW_WIDTH), C=DN_CHUNK, TL=_tile(L, GDN_TILE))
    steps = {win // d for win, d in DILATIONS}
    assert len(steps) == 1, "every dilation looks back the same number of class rows"
    p_ob = _attn_prompt((qb1, qb4, qb16), p_zb, dils=dils, blk=steps.pop(), rt=_tile(L, 1024))
    yp = _out(xp, p_oa, p_ob, w_out16, fg_row, tm=tm, name="out_prompt")
    prompt = (yp, raw_last[:, HALO - hist:, :], p_s,
              kv_last[:, :, :SW_WIDTH].reshape(Bp, keep, SW_HEADS, SW_HD),
              kv_last[:, :, SW_WIDTH:].reshape(Bp, keep, SW_HEADS, SW_HD))

    oa, s_new = _gdn_sample(qkvn, gb, front(za), s0, ng_row, C=group, n=_tile(B * group, GDN_TILE))
    oa = oa.reshape(B, group, DN_WIDTH)[:, group - T:].reshape(1, B * T, DN_WIDTH)
    ys = _out(rows, oa, ob.reshape(1, B * T, SW_WIDTH).astype(BF16), w_out16, fg_row,
              tm=_tile(B * T, 256), name="out_sample")
    sample = (ys.reshape(B, T, D_MODEL), jnp.concatenate([conv_state, raw], axis=1)[:, -hist:], s_new,
              jnp.transpose(kt_new.reshape(B, SW_HEADS, SW_HD, W), (0, 3, 1, 2)),
              jnp.transpose(vt_new.reshape(B, SW_HEADS, SW_HD, W), (0, 3, 1, 2)))
    return prompt, sample


def kernel(x_prompt, x_sample, state_conv, state_delta, cache_k_win, cache_v_win, norm_g, final_norm_g,
           w_in, conv_w, a_log, dt_bias, dn_norm_g, w_out):
    assert norm_g.shape[0] == 1, "single layer"
    w = w_in[0]
    o1 = CONV_CH + DN_WIDTH
    o3 = o1 + 2 * DN_HEADS
    o4 = o3 + 3 * SW_WIDTH
    wp = jnp.concatenate(
        [w[:, :o1], w[:, o3:], w[:, o1:o3], jnp.zeros((D_MODEL, LANES - 2 * DN_HEADS), w.dtype)],
        axis=1).astype(BF16)
    assert wp.shape[1] == W_COLS and o4 - o3 == 3 * SW_WIDTH
    g_row = norm_g[0].reshape(1, D_MODEL)
    fg_row = final_norm_g.reshape(1, D_MODEL)
    alog_row = _pad_lanes(a_log[0])
    dtb_row = _pad_lanes(dt_bias[0])
    ng_row = dn_norm_g[0].reshape(1, DN_DV)
    w_out16 = w_out[0].astype(BF16)
    cw = conv_w[0]
    common = (wp, g_row, cw, alog_row, dtb_row, ng_row, w_out16, fg_row)
    (yp, pc, ps, pk, pv), (ys, sc, ss, sk, sv) = _layers(
        x_prompt, x_sample, state_conv[0], state_delta[0], cache_k_win[0], cache_v_win[0], *common)
    return (yp, ys, pc[None], ps[None], pk[None], pv[None], sc[None], ss[None], sk[None], sv[None])
```

```python
import functools

import numpy as np
import jax
import jax.numpy as jnp
from jax import lax
from jax.experimental import pallas as pl
from jax.experimental.pallas import tpu as pltpu

F32 = jnp.float32
BF16 = jnp.bfloat16

D_MODEL = 1024
DN_HEADS = 4
DN_DK = 128
DN_DV = 128
QK_W = DN_HEADS * DN_DK
DN_WIDTH = DN_HEADS * DN_DV
CONV_W = 4
CONV_CH = 2 * QK_W + DN_WIDTH
DN_CHUNK = 64
SW_HEADS = 8
SW_HD = 64
SW_WIDTH = SW_HEADS * SW_HD
DILATIONS = ((128, 1), (512, 4), (2048, 16))
WIN_MAX = 2048
NORM_EPS = 1e-6

LANES = 128
SUBLANES = 8
VMEM_LIMIT = 56 * 1024 * 1024

C_RAW = 0
C_ZA = CONV_CH
C_QKVB = C_ZA + DN_WIDTH
C_ZB = C_QKVB + 3 * SW_WIDTH
C_AB = C_ZB + SW_WIDTH
W_COLS = C_AB + LANES
HALO = SUBLANES
GDN_TILE = DN_HEADS * DN_CHUNK
PROJ_ROWS = 512
SAMPLE_ROWS = 256
ATTN_ROWS = 1024


def _dot(a, b):
    return lax.dot_general(a, b, (((a.ndim - 1,), (0,)), ((), ())), preferred_element_type=F32)


def _dot_nt(a, b):
    return lax.dot_general(a, b, (((1,), (1,)), ((), ())), preferred_element_type=F32)


def _dot_tn(a, b):
    return lax.dot_general(a, b, (((0,), (0,)), ((), ())), preferred_element_type=F32)


def _bdot(a, b):
    return lax.dot_general(a, b, (((2,), (1,)), ((0,), (0,))), preferred_element_type=F32)


def _bdot_tn(a, b):
    return lax.dot_general(a, b, (((1,), (1,)), ((0,), (0,))), preferred_element_type=F32)


def _dot_exact_lhs(a16, x):
    hi = x.astype(BF16)
    r1 = x - hi.astype(F32)
    mid = r1.astype(BF16)
    lo = (r1 - mid.astype(F32)).astype(BF16)
    return _dot(a16, lo) + _dot(a16, mid) + _dot(a16, hi)


def _sigmoid(x):
    return 1.0 / (1.0 + jnp.exp(-x))


def _silu(x):
    return x * _sigmoid(x)


def _conv_norm_gates(ext_scr, cw_ref, alog_ref, dtb_ref, qkvn_ref, gb_ref, rows, live_from):
    ch = LANES if rows % LANES == 0 else rows
    cw = cw_ref[...]
    a_neg = -jnp.exp(alog_ref[...])
    dtb = dtb_ref[...]
    lane = lax.broadcasted_iota(jnp.int32, (ch, LANES), 1)

    def chunk(c, carry):
        r0 = c * ch
        out = pl.ds(r0, ch)
        for s in range(CONV_CH // LANES):
            sl = slice(s * LANES, (s + 1) * LANES)
            win = ext_scr[pl.ds(r0, ch + HALO), sl]
            y = cw[0:1, sl] * win[HALO - 3:HALO - 3 + ch]
            y = y + cw[1:2, sl] * win[HALO - 2:HALO - 2 + ch]
            y = y + cw[2:3, sl] * win[HALO - 1:HALO - 1 + ch]
            y = y + cw[3:4, sl] * win[HALO:HALO + ch]
            v = _silu(y)
            if s < DN_HEADS:
                v = v * lax.rsqrt(jnp.sum(v * v, -1, keepdims=True) + NORM_EPS) * (DN_DK ** -0.5)
            elif s < 2 * DN_HEADS:
                v = v * lax.rsqrt(jnp.sum(v * v, -1, keepdims=True) + NORM_EPS)
            qkvn_ref[0, out, sl] = v.astype(qkvn_ref.dtype)
        ab = gb_ref[0, out, :]
        xg = ab + dtb
        softplus = jnp.maximum(xg, 0.0) + jnp.log1p(jnp.exp(-jnp.abs(xg)))
        gb = jnp.where(lane < DN_HEADS, a_neg * softplus, _sigmoid(ab))
        if live_from is not None:
            group, first = live_from
            rowi = lax.broadcasted_iota(jnp.int32, (ch, LANES), 0) + r0
            gb = jnp.where((rowi & (group - 1)) >= first, gb, 0.0)
        gb_ref[0, out, :] = gb
        return carry

    for c in range(rows // ch):
        chunk(c, 0)


def _normed(x_ref, g_ref, h_scr):
    rows = x_ref.shape[1]
    ch = LANES if rows % LANES == 0 else rows
    g = g_ref[...]

    for c in range(rows // ch):
        rws = pl.ds(c * ch, ch)
        x = x_ref[0, rws, :]
        ms = jnp.mean(x * x, axis=-1, keepdims=True)
        h_scr[rws, :] = ((x * lax.rsqrt(ms + NORM_EPS)) * g).astype(BF16)
    return h_scr[...]


def _proj_sample_kernel(x_ref, g_ref, w_ref, raw_ref, za_ref, zb_ref, ab_ref, qb_ref, h_scr):
    h = _normed(x_ref, g_ref, h_scr)
    raw_ref[0] = _dot(h, w_ref[:, C_RAW:C_ZA])
    za_ref[0] = _dot(h, w_ref[:, C_ZA:C_QKVB])
    zb_ref[0] = _dot(h, w_ref[:, C_ZB:C_AB])
    ab_ref[0] = _dot(h, w_ref[:, C_AB:W_COLS])
    qb_ref[0] = _dot(h, w_ref[:, C_QKVB:C_ZB])


def _proj_prompt_kernel(x_ref, g_ref, w_ref, cw_ref, alog_ref, dtb_ref,
                        qkvn_ref, gb_ref, za_ref, zb_ref, rawlast_ref, klast_ref, vlast_ref, *rest, dils, tm):
    nd = len(dils)
    qb_refs = rest[:nd]
    cls_scrs = list(rest[nd:2 * nd - 1]) + [None]
    ext_scr, h_scr = rest[2 * nd - 1:]

    @pl.when(pl.program_id(1) == 0)
    def _():
        ext_scr[0:HALO, :] = jnp.zeros((HALO, CONV_CH), F32)

    _normed(x_ref, g_ref, h_scr)
    qkvb = _dot(h_scr[...], w_ref[:, C_QKVB:C_ZB])
    ext_scr[HALO:HALO + tm, :] = _dot(h_scr[...], w_ref[:, C_RAW:C_ZA])
    za_ref[0] = _dot(h_scr[...], w_ref[:, C_ZA:C_QKVB]).astype(za_ref.dtype)
    zb_ref[0] = _dot(h_scr[...], w_ref[:, C_ZB:C_AB]).astype(zb_ref.dtype)
    gb_ref[0] = _dot(h_scr[...], w_ref[:, C_AB:W_COLS])

    klast_ref[0] = qkvb[:, SW_WIDTH:2 * SW_WIDTH]
    vlast_ref[0] = qkvb[:, 2 * SW_WIDTH:]
    nslab = 3 * SW_WIDTH // LANES
    assert dils[0] == 1
    qb_refs[0][0, 0] = qkvb.astype(BF16)
    for j in range(nslab):
        cls_scrs[0][j] = qkvb[:, j * LANES:(j + 1) * LANES]
    for i in range(1, nd):
        d, prev = dils[i], dils[i - 1]
        ratio, n_d, n_prev = d // prev, tm // d, tm // prev
        assert d % prev == 0
        for r in range(d):
            for j in range(nslab):
                v = cls_scrs[i - 1][j, pl.ds((r % prev) * n_prev + r // prev, n_d, stride=ratio), :]
                qb_refs[i][0, r, :, j * LANES:(j + 1) * LANES] = v.astype(BF16)
                if cls_scrs[i] is not None:
                    cls_scrs[i][j, r * n_d:(r + 1) * n_d, :] = v

    _conv_norm_gates(ext_scr, cw_ref, alog_ref, dtb_ref, qkvn_ref, gb_ref, tm, None)
    last = ext_scr[tm:tm + HALO, :]
    rawlast_ref[0] = last
    ext_scr[0:HALO, :] = last


def _const_spec(shape):
    zeros = (0,) * len(shape)
    return pl.BlockSpec(shape, lambda *_: zeros)


def _proj_prompt(x, g, w, cw, alog_row, dtb_row, *, dils, keep, tm):
    B, L, _ = x.shape
    row = lambda b, t: (b, t, 0)
    first = (L - keep) // tm
    out_shape = [jax.ShapeDtypeStruct((B, L, CONV_CH), F32),
                 jax.ShapeDtypeStruct((B, L, LANES), F32),
                 jax.ShapeDtypeStruct((B, L, DN_WIDTH), BF16),
                 jax.ShapeDtypeStruct((B, L, SW_WIDTH), BF16),
                 jax.ShapeDtypeStruct((B, HALO, CONV_CH), F32),
                 jax.ShapeDtypeStruct((B, keep, SW_WIDTH), F32),
                 jax.ShapeDtypeStruct((B, keep, SW_WIDTH), F32)]
    out_specs = [pl.BlockSpec((1, tm, CONV_CH), row),
                 pl.BlockSpec((1, tm, LANES), row),
                 pl.BlockSpec((1, tm, DN_WIDTH), row),
                 pl.BlockSpec((1, tm, SW_WIDTH), row),
                 pl.BlockSpec((1, HALO, CONV_CH), lambda b, t: (b, 0, 0)),
                 pl.BlockSpec((1, tm, SW_WIDTH), lambda b, t: (b, jnp.maximum(t - first, 0), 0)),
                 pl.BlockSpec((1, tm, SW_WIDTH), lambda b, t: (b, jnp.maximum(t - first, 0), 0))]
    for d in dils:
        out_shape.append(jax.ShapeDtypeStruct((B, d, L // d, 3 * SW_WIDTH), BF16))
        out_specs.append(pl.BlockSpec((1, d, tm // d, 3 * SW_WIDTH), lambda b, t: (b, 0, t, 0)))
    return pl.pallas_call(
        functools.partial(_proj_prompt_kernel, dils=dils, tm=tm),
        grid=(B, L // tm),
        in_specs=[pl.BlockSpec((1, tm, D_MODEL), row),
                  _const_spec((1, D_MODEL)),
                  pl.BlockSpec((D_MODEL, W_COLS), lambda b, t: (0, 0), pipeline_mode=pl.Buffered(1)),
                  _const_spec((CONV_W, CONV_CH)),
                  _const_spec((1, LANES)),
                  _const_spec((1, LANES))],
        out_specs=out_specs,
        out_shape=out_shape,
        scratch_shapes=[pltpu.VMEM((3 * SW_WIDTH // LANES, tm, LANES), F32)] * (len(dils) - 1)
        + [pltpu.VMEM((tm + HALO, CONV_CH), F32), pltpu.VMEM((tm, D_MODEL), BF16)],
        compiler_params=pltpu.CompilerParams(
            dimension_semantics=("arbitrary", "arbitrary"), vmem_limit_bytes=VMEM_LIMIT),
        name="proj_prompt",
    )(x, g, w, cw, alog_row, dtb_row)


def _proj_sample(x, g, w, *, tm):
    B, L, _ = x.shape
    row = lambda b, t: (b, t, 0)
    widths = (CONV_CH, DN_WIDTH, SW_WIDTH, LANES, 3 * SW_WIDTH)
    return pl.pallas_call(
        _proj_sample_kernel,
        grid=(B, L // tm),
        in_specs=[pl.BlockSpec((1, tm, D_MODEL), row), _const_spec((1, D_MODEL)),
                  _const_spec((D_MODEL, W_COLS))],
        out_specs=[pl.BlockSpec((1, tm, c), row) for c in widths],
        out_shape=[jax.ShapeDtypeStruct((B, L, c), F32) for c in widths],
        scratch_shapes=[pltpu.VMEM((tm, D_MODEL), BF16)],
        compiler_params=pltpu.CompilerParams(
            dimension_semantics=("arbitrary", "arbitrary"), vmem_limit_bytes=VMEM_LIMIT),
        name="proj_sample",
    )(x, g, w)


def _gdn_pre_kernel(ext_ref, ab_ref, cw_ref, alog_ref, dtb_ref, qkvn_ref, gb_ref, ext_scr, *, rows, group, t_new):
    ext_scr[0:HALO, :] = jnp.zeros((HALO, CONV_CH), F32)
    ext_scr[HALO:HALO + rows, :] = ext_ref[0]
    gb_ref[0] = ab_ref[0]
    _conv_norm_gates(ext_scr, cw_ref, alog_ref, dtb_ref, qkvn_ref, gb_ref, rows, (group, group - t_new))


def _gdn_pre(ext, ab, cw, alog_row, dtb_row, *, group, t_new):
    _, rows, _ = ext.shape
    return pl.pallas_call(
        functools.partial(_gdn_pre_kernel, rows=rows, group=group, t_new=t_new),
        grid=(1,),
        in_specs=[_const_spec((1, rows, CONV_CH)), _const_spec((1, rows, LANES)),
                  _const_spec((CONV_W, CONV_CH)), _const_spec((1, LANES)), _const_spec((1, LANES))],
        out_specs=[_const_spec((1, rows, CONV_CH)), _const_spec((1, rows, LANES))],
        out_shape=[jax.ShapeDtypeStruct((1, rows, CONV_CH), F32), jax.ShapeDtypeStruct((1, rows, LANES), F32)],
        scratch_shapes=[pltpu.VMEM((rows + HALO, CONV_CH), F32)],
        compiler_params=pltpu.CompilerParams(
            dimension_semantics=("arbitrary",), vmem_limit_bytes=VMEM_LIMIT),
        name="gdn_pre_sample",
    )(ext, ab, cw, alog_row, dtb_row)


def _block_masks(n, c):
    r = np.arange(n)[:, None]
    cc = np.arange(n)[None, :]
    x = r ^ cc
    same = x < c
    incl = same & (cc <= r)
    levels = []
    k = 1
    while (2 << k) <= c:
        levels.append((x >> k) == 1)
        k += 1
    masks = np.stack([r == cc, incl, (x >> 1) == 0]).astype(np.float32)
    lmask = np.concatenate([incl, same] + levels, axis=0).astype(np.float32)
    return jnp.asarray(masks), jnp.asarray(lmask, dtype=BF16)


def _gdn_tiles(tiles, masks_ref, lmask_ref):
    n = tiles[0][0].shape[0]
    nt = len(tiles)
    nlev = lmask_ref.shape[0] // n - 2
    eye = masks_ref[0]
    incl = masks_ref[1]
    g_b = [jnp.broadcast_to(t[3], (n, LANES)) for t in tiles]
    beta_b = [jnp.broadcast_to(t[4], (n, LANES)) for t in tiles]
    cs = _dot_exact_lhs(lmask_ref[0:2 * n, :], jnp.concatenate(g_b, axis=1))
    gc_b = [cs[:n, i * LANES:(i + 1) * LANES] for i in range(nt)]
    gl_b = [cs[n:, i * LANES:(i + 1) * LANES] for i in range(nt)]
    kb = [tiles[i][1] * beta_b[i] for i in range(nt)]
    k16 = [t[1].astype(BF16) for t in tiles]
    kk = [_dot_nt(kb[i].astype(BF16), k16[i]) for i in range(nt)]
    qk = [_dot_nt(tiles[i][0].astype(BF16), k16[i]) for i in range(nt)]
    a16, attn, t = [], [], []
    for i in range(nt):
        gc_sq = jnp.concatenate([gc_b[i]] * (n // LANES), axis=1)
        gc_row = jnp.sum(gc_sq * eye, axis=0, keepdims=True)
        dec = jnp.exp(jnp.minimum(gc_sq - gc_row, 0.0)) * incl
        a = kk[i] * (dec - eye)
        t.append(eye - a * masks_ref[2])
        a16.append(a.astype(BF16))
        attn.append((qk[i] * dec).astype(BF16))
    for lv in range(nlev):
        t16 = [x.astype(BF16) for x in t]
        m16 = lmask_ref[(2 + lv) * n:(3 + lv) * n, :]
        p = [_dot(t16[i], a16[i] * m16) for i in range(nt)]
        t = [t[i] - _dot(p[i].astype(BF16), t16[i]) for i in range(nt)]
    e_gc = [jnp.exp(x) for x in gc_b]
    sol = [_dot(t[i].astype(BF16),
                jnp.concatenate([tiles[i][2] * beta_b[i], kb[i] * e_gc[i]], axis=1).astype(BF16))
           for i in range(nt)]
    return [(sol[i][:, :DN_DV], sol[i][:, DN_DV:], tiles[i][0] * e_gc[i],
             tiles[i][1] * jnp.exp(gl_b[i] - gc_b[i]), attn[i], gl_b[i]) for i in range(nt)]


def _out_norm(o, ng, za):
    return ((o * lax.rsqrt(jnp.mean(o * o, -1, keepdims=True) + NORM_EPS)) * ng) * _silu(za.astype(F32))


def _gdn_prompt_kernel(qkvn_ref, gb_ref, za_ref, ng_ref, masks_ref, lmask_ref, kt_ref, vt_ref, new_ref,
                       zbs_ref, o_ref, s_out_ref, kto_ref, vto_ref, obs_ref, s_scr, knew_scr, vnew_scr,
                       *, C, TL, G):
    H = DN_HEADS

    @pl.when(pl.program_id(1) == 0)
    def _():
        s_scr[...] = jnp.zeros_like(s_scr)

    @pl.when((pl.program_id(0) == 0) & (pl.program_id(1) == 0))
    def _():
        knew_scr[...] = jnp.zeros_like(knew_scr)
        vnew_scr[...] = jnp.zeros_like(vnew_scr)

    finish_sample = []
    for u in range(kt_ref.shape[0]):
        _shift_window(kt_ref, u, new_ref[u, :, SW_WIDTH:2 * SW_WIDTH], knew_scr, kto_ref)
        _shift_window(vt_ref, u, new_ref[u, :, 2 * SW_WIDTH:], vnew_scr, vto_ref)
        finish_sample.append(_sample_attention(new_ref[u], zbs_ref[u], kt_ref, vt_ref, u, knew_scr[...],
                                               vnew_scr[...], obs_ref, DILATIONS))

    ng = ng_ref[...]

    def group(gi, carry):
        base = gi * (G * C) if isinstance(gi, int) else pl.multiple_of(gi * (G * C), G * C)
        rows = [pl.ds(base + i * C, C) for i in range(G)]

        def stack(ref, rws, off):
            return jnp.concatenate(
                [ref[0, rws, off + h * LANES:off + (h + 1) * LANES] for h in range(H)], axis=0)

        tiles, zas = [], []
        for rws in rows:
            gbc = gb_ref[0, rws, :]
            g_col = jnp.concatenate([gbc[:, h:h + 1] for h in range(H)], axis=0)
            beta_col = jnp.concatenate([gbc[:, H + h:H + h + 1] for h in range(H)], axis=0)
            zas.append(stack(za_ref, rws, 0))
            tiles.append((stack(qkvn_ref, rws, 0).astype(F32), stack(qkvn_ref, rws, QK_W).astype(F32),
                          stack(qkvn_ref, rws, 2 * QK_W).astype(F32), g_col, beta_col))
        tiles = _gdn_tiles(tiles, masks_ref, lmask_ref)
        s = [s_scr[h] for h in range(H)]
        outs = []
        for (u, w, q_dec, k_dec, attn, gl_b), za in zip(tiles, zas):
            ws, qs = [], []
            for h in range(H):
                blk = slice(h * C, (h + 1) * C)
                r = _dot(jnp.concatenate([w[blk], q_dec[blk]], axis=0).astype(BF16), s[h].astype(BF16))
                ws.append(r[:C])
                qs.append(r[C:])
            v_new = u - jnp.concatenate(ws, axis=0)
            vn16 = v_new.astype(BF16)
            o = jnp.concatenate(qs, axis=0) + _dot(attn.astype(BF16), vn16)
            for h in range(H):
                blk = slice(h * C, (h + 1) * C)
                gl_s = jnp.broadcast_to(jnp.exp(gl_b[h * C:h * C + 1, :]), (DN_DK, DN_DV))
                s[h] = s[h] * gl_s + _dot_tn(k_dec[blk].astype(BF16), vn16[blk])
            outs.append(_out_norm(o, ng, za))
        for rws, out in zip(rows, outs):
            for h in range(H):
                o_ref[0, rws, h * DN_DV:(h + 1) * DN_DV] = out[h * C:(h + 1) * C].astype(o_ref.dtype)
        for h in range(H):
            s_scr[h] = s[h]
        return carry

    for gi in range(TL // (G * C)):
        group(gi, 0)
    for finish in finish_sample:
        finish()
    s_out_ref[0] = s_scr[...]


def _gdn_prompt(qkvn, gb, za, ng_row, kt, vt, new_rows, zbs, *, C, TL):
    B, L, _ = qkvn.shape
    n = DN_HEADS * C
    nt = L // TL
    Bs, T, _ = new_rows.shape
    W = kt.shape[-1]
    assert Bs % (B * nt) == 0
    per = Bs // (B * nt)
    masks, lmask = _block_masks(n, C)
    row = lambda b, t: (b, t, 0)
    share = lambda b, t: (b * nt + t, 0, 0)
    window = pl.BlockSpec((per, SW_WIDTH, W), share)
    return pl.pallas_call(
        functools.partial(_gdn_prompt_kernel, C=C, TL=TL, G=4 if TL % (4 * C) == 0 else 1),
        grid=(B, nt),
        in_specs=[pl.BlockSpec((1, TL, CONV_CH), row),
                  pl.BlockSpec((1, TL, LANES), row),
                  pl.BlockSpec((1, TL, DN_WIDTH), row),
                  _const_spec((1, DN_DV)), _const_spec(masks.shape), _const_spec(lmask.shape),
                  window, window, pl.BlockSpec((per, T, 3 * SW_WIDTH), share),
                  pl.BlockSpec((per, T, SW_WIDTH), share)],
        out_specs=[pl.BlockSpec((1, TL, DN_WIDTH), row),
                   pl.BlockSpec((1, DN_HEADS, DN_DK, DN_DV), lambda b, t: (b, 0, 0, 0)),
                   window, window, pl.BlockSpec((per, T, SW_WIDTH), share)],
        out_shape=[jax.ShapeDtypeStruct((B, L, DN_WIDTH), BF16),
                   jax.ShapeDtypeStruct((B, DN_HEADS, DN_DK, DN_DV), F32),
                   jax.ShapeDtypeStruct(kt.shape, F32), jax.ShapeDtypeStruct(vt.shape, F32),
                   jax.ShapeDtypeStruct((Bs, T, SW_WIDTH), F32)],
        scratch_shapes=[pltpu.VMEM((DN_HEADS, DN_DK, DN_DV), F32),
                        pltpu.VMEM((LANES, SW_WIDTH), F32), pltpu.VMEM((LANES, SW_WIDTH), F32)],
        compiler_params=pltpu.CompilerParams(
            dimension_semantics=("arbitrary", "arbitrary"), vmem_limit_bytes=VMEM_LIMIT),
        name="gdn_prompt",
    )(qkvn, gb, za, ng_row, masks, lmask, kt, vt, new_rows, zbs)


def _gdn_sample_kernel(qkvn_ref, gb_ref, za_ref, s0_ref, ng_ref, masks_ref, lmask_ref, o_ref, s_out_ref, *, C):
    n = qkvn_ref.shape[1]
    nb = n // C
    ng = ng_ref[...]
    split = lambda a: a.reshape(nb, C, a.shape[-1])
    tiles = _gdn_tiles(
        [(qkvn_ref[0, :, h * LANES:(h + 1) * LANES],
          qkvn_ref[0, :, QK_W + h * LANES:QK_W + (h + 1) * LANES],
          qkvn_ref[0, :, 2 * QK_W + h * LANES:2 * QK_W + (h + 1) * LANES],
          gb_ref[0, :, h:h + 1], gb_ref[0, :, DN_HEADS + h:DN_HEADS + h + 1]) for h in range(DN_HEADS)],
        masks_ref, lmask_ref)
    for h in range(DN_HEADS):
        sl = slice(h * LANES, (h + 1) * LANES)
        u, w, q_dec, k_dec, attn, gl_b = tiles[h]
        s0 = s0_ref[:, h]
        r = _bdot(jnp.concatenate([split(w), split(q_dec)], axis=1).astype(BF16), s0.astype(BF16))
        v_new = split(u) - r[:, :C]
        vn16 = v_new.astype(BF16)
        o = r[:, C:].reshape(n, DN_DV) + _dot(attn.astype(BF16), vn16.reshape(n, DN_DV))
        gl = split(jnp.exp(gl_b))[:, 0:1, :]
        s_out_ref[:, h] = s0 * gl + _bdot_tn(split(k_dec).astype(BF16), vn16)
        o_ref[0, :, sl] = _out_norm(o, ng, za_ref[0, :, sl]).astype(o_ref.dtype)


def _gdn_sample(qkvn, gb, za, s0, ng_row, *, C, n):
    _, rows, _ = qkvn.shape
    nb = n // C
    masks, lmask = _block_masks(n, C)
    row = lambda i: (0, i, 0)
    state = pl.BlockSpec((nb, DN_HEADS, DN_DK, DN_DV), lambda i: (i, 0, 0, 0))
    return pl.pallas_call(
        functools.partial(_gdn_sample_kernel, C=C),
        grid=(rows // n,),
        in_specs=[pl.BlockSpec((1, n, CONV_CH), row), pl.BlockSpec((1, n, LANES), row),
                  pl.BlockSpec((1, n, DN_WIDTH), row), state,
                  _const_spec((1, DN_DV)), _const_spec(masks.shape), _const_spec(lmask.shape)],
        out_specs=[pl.BlockSpec((1, n, DN_WIDTH), row), state],
        out_shape=[jax.ShapeDtypeStruct((1, rows, DN_WIDTH), BF16), jax.ShapeDtypeStruct(s0.shape, F32)],
        compiler_params=pltpu.CompilerParams(
            dimension_semantics=("arbitrary",), vmem_limit_bytes=VMEM_LIMIT),
        name="gdn_sample",
    )(qkvn, gb, za, s0, ng_row, masks, lmask)


def _swa_blocks(items):
    mq = items[0][0].shape[0]
    npair = SW_WIDTH // LANES
    nh = 2 * npair
    lo_half = lax.broadcasted_iota(jnp.int32, (mq, LANES), 1) < SW_HD
    scale = SW_HD ** -0.5
    half_masks = (jnp.where(lo_half, scale, 0.0).astype(BF16), jnp.where(lo_half, 0.0, scale).astype(BF16))
    scores = []
    for q, kk, _, _ in items:
        for p in range(npair):
            sl = slice(p * LANES, (p + 1) * LANES)
            for half in range(2):
                scores.append(_dot_nt(q[:, sl] * half_masks[half], kk[:, sl]))
    probs, inv_l, lses = [], [], []
    for i, s in enumerate(scores):
        s = jnp.where(items[i // nh][3], s, -jnp.inf)
        m = jnp.max(s, axis=-1, keepdims=True)
        pe = jnp.exp(s - m)
        l = jnp.sum(pe, axis=-1, keepdims=True)
        probs.append(pe.astype(BF16))
        inv_l.append(1.0 / l)
        lses.append(jnp.broadcast_to(m + jnp.log(l), (mq, LANES)))
    pvs = [_dot(probs[i], items[i // nh][2][:, ((i % nh) // 2) * LANES:((i % nh) // 2 + 1) * LANES])
           for i in range(len(scores))]
    out = []
    for b in range(len(items)):
        e = [b * nh + 2 * p for p in range(npair)]
        out.append([(jnp.where(lo_half, pvs[i] * inv_l[i], pvs[i + 1] * inv_l[i + 1]),
                     jnp.where(lo_half, lses[i], lses[i + 1])) for i in e])
    return out


def _swa_geometry(rt, d, blk):
    qn = rt // d
    sb = min(blk, qn)
    return qn, sb, blk // sb


def _swa_wide_kernel(q_ref, kp_ref, kc_ref, vp_ref, vc_ref, o_ref, lse_ref, *, d, blk, gs):
    qi = lax.broadcasted_iota(jnp.int32, (blk, 2 * blk), 0)
    kj = lax.broadcasted_iota(jnp.int32, (blk, 2 * blk), 1)
    band = (kj >= qi) & (kj <= qi + blk) & ((kj >= blk) | (pl.program_id(1) > 0))
    npair = SW_WIDTH // LANES

    def group(g, carry):
        classes = [g * gs + i for i in range(gs)]
        items = [(q_ref[0, r], jnp.concatenate([kp_ref[0, r], kc_ref[0, r]], axis=0),
                  jnp.concatenate([vp_ref[0, r], vc_ref[0, r]], axis=0), band) for r in classes]
        for r, res in zip(classes, _swa_blocks(items)):
            rows = pl.ds(r, blk, stride=d)
            for p in range(npair):
                o_ref[0, p, rows, :] = res[p][0]
                lse_ref[0, p, rows, :] = res[p][1]
        return carry

    lax.fori_loop(0, d // gs, group, 0)


def _swa_wide(qb, *, d, blk):
    B, _, n, _ = qb.shape
    cur = lambda c: (lambda b, i: (b, 0, i, c))
    prev = lambda c: (lambda b, i: (b, 0, jnp.maximum(i - 1, 0), c))
    spec = lambda f: pl.BlockSpec((1, d, blk, SW_WIDTH), f)
    npair = SW_WIDTH // LANES
    out_spec = pl.BlockSpec((1, npair, blk * d, LANES), lambda b, i: (b, 0, i, 0))
    return pl.pallas_call(
        functools.partial(_swa_wide_kernel, d=d, blk=blk, gs=2),
        grid=(B, n // blk),
        in_specs=[spec(cur(0)), spec(prev(1)), spec(cur(1)), spec(prev(2)), spec(cur(2))],
        out_specs=[out_spec, out_spec],
        out_shape=[jax.ShapeDtypeStruct((B, npair, n * d, LANES), F32)] * 2,
        compiler_params=pltpu.CompilerParams(
            dimension_semantics=("arbitrary", "arbitrary"), vmem_limit_bytes=VMEM_LIMIT),
        name="swa_d%d" % d,
    )(qb, qb, qb, qb, qb)


def _attn_kernel(*refs, dils, n_ext, rt, blk):
    nbr = len(dils)
    zb_ref, ob_ref, o_scr, lse_scr = refs[-4:]
    ext = refs[-4 - 2 * n_ext:-4]
    t = pl.program_id(1)
    npair = SW_WIDTH // LANES
    idx = 0
    for bi, d in enumerate(dils):
        qn, sb, npiece = _swa_geometry(rt, d, blk)
        q_ref, k_ref, v_ref = refs[idx:idx + 3]
        kp = refs[idx + 3:idx + 3 + npiece]
        vp = refs[idx + 3 + npiece:idx + 3 + 2 * npiece]
        idx += 3 + 2 * npiece
        nk = blk + sb
        qi = lax.broadcasted_iota(jnp.int32, (sb, nk), 0)
        kj = lax.broadcasted_iota(jnp.int32, (sb, nk), 1)
        band = (kj >= qi) & (kj <= qi + blk)
        band0 = band & (kj >= blk - t * qn)

        nsub = qn // sb

        def first(r, q_ref=q_ref, k_ref=k_ref, v_ref=v_ref, kp=kp, vp=vp, sb=sb, band0=band0):
            kk = jnp.concatenate([x[0, r] for x in kp] + [k_ref[0, r, 0:sb, :]], axis=0)
            vv = jnp.concatenate([x[0, r] for x in vp] + [v_ref[0, r, 0:sb, :]], axis=0)
            return r, 0, (q_ref[0, r, 0:sb, :], kk, vv, band0)

        def later(r, j, q_ref=q_ref, k_ref=k_ref, v_ref=v_ref, sb=sb, band=band):
            aligned = (lambda x: x) if isinstance(j, int) else (lambda x: pl.multiple_of(x, sb))
            keys = pl.ds(aligned((j - 1) * sb), 2 * sb)
            return r, j, (q_ref[0, r, pl.ds(aligned(j * sb), sb), :],
                          k_ref[0, r, keys, :], v_ref[0, r, keys, :], band)

        def run(blocks, bi=bi, d=d, sb=sb):
            for (r, j, _), res in zip(blocks, _swa_blocks([b[2] for b in blocks])):
                start = j * (sb * d) + r
                rows = pl.ds(start, sb, stride=d) if d > 1 else pl.ds(start, sb)
                for p in range(npair):
                    o_scr[bi, p, rows, :] = res[p][0]
                    lse_scr[bi, p, rows, :] = res[p][1]

        if nsub == 1:
            gs = 4
            assert d % gs == 0
            lax.fori_loop(0, d // gs, lambda g, c, run=run, first=first:
                          (run([first(gs * g + i) for i in range(gs)]), c)[1], 0)
        else:
            assert nsub % 2 == 0

            def do_class(r, run=run, first=first, later=later, nsub=nsub):
                run([first(r), later(r, 1)])
                if nsub > 2:
                    lax.fori_loop(1, nsub // 2, lambda g, c:
                                  (run([later(r, 2 * g), later(r, 2 * g + 1)]), c)[1], 0)

            if d == 1:
                do_class(0)
            else:
                lax.fori_loop(0, d, lambda r, c, f=do_class: (f(r), c)[1], 0)

    def merge(c, carry):
        rows = pl.ds(pl.multiple_of(c * blk, blk), blk)
        zb = zb_ref[0, rows, :].astype(F32)
        for p in range(npair):
            lses = [lse_scr[i, p, rows, :] for i in range(nbr)]
            lses += [ext[2 * i + 1][0, p, rows, :] for i in range(n_ext)]
            outs = [o_scr[i, p, rows, :] for i in range(nbr)] + [ext[2 * i][0, p, rows, :] for i in range(n_ext)]
            mx = functools.reduce(jnp.maximum, lses)
            wts = [jnp.exp(l - mx) for l in lses]
            numer = sum(w * o for w, o in zip(wts, outs))
            sl = slice(p * LANES, (p + 1) * LANES)
            ob_ref[0, rows, sl] = ((numer / sum(wts)) * _silu(zb[:, sl])).astype(ob_ref.dtype)
        return carry

    lax.fori_loop(0, rt // blk, merge, 0)


def _attn_prompt(qbs, ext, zb, *, dils, blk, rt):
    B, L, _ = zb.shape
    args, specs = [], []
    for qb, d in zip(qbs, dils):
        qn, sb, npiece = _swa_geometry(rt, d, blk)
        for c in range(3):
            specs.append(pl.BlockSpec((1, d, qn, SW_WIDTH), lambda b, t, c=c: (b, 0, t, c)))
            args.append(qb)
        for c in (1, 2):
            for pi in range(npiece):
                back, per = npiece - pi, qn // sb
                specs.append(pl.BlockSpec(
                    (1, d, sb, SW_WIDTH),
                    lambda b, t, c=c, back=back, per=per: (b, 0, jnp.maximum(t * per - back, 0), c)))
                args.append(qb)
    npair = SW_WIDTH // LANES
    for a in ext:
        specs.append(pl.BlockSpec((1, npair, rt, LANES), lambda b, t: (b, 0, t, 0)))
        args.append(a)
    specs.append(pl.BlockSpec((1, rt, SW_WIDTH), lambda b, t: (b, t, 0)))
    args.append(zb)
    return pl.pallas_call(
        functools.partial(_attn_kernel, dils=dils, n_ext=len(ext) // 2, rt=rt, blk=blk),
        grid=(B, L // rt),
        in_specs=specs,
        out_specs=pl.BlockSpec((1, rt, SW_WIDTH), lambda b, t: (b, t, 0)),
        out_shape=jax.ShapeDtypeStruct((B, L, SW_WIDTH), BF16),
        scratch_shapes=[pltpu.VMEM((len(dils), npair, rt, LANES), F32),
                        pltpu.VMEM((len(dils), npair, rt, LANES), F32)],
        compiler_params=pltpu.CompilerParams(
            dimension_semantics=("arbitrary", "arbitrary"), vmem_limit_bytes=VMEM_LIMIT),
        name="attn_prompt",
    )(*args)


def _shift_window(src_ref, u, new_rows, new_scr, dst_ref):
    T = new_rows.shape[0]
    ncol = src_ref.shape[-1] // LANES
    shift = LANES - T
    new_scr[0:T, :] = new_rows
    new_t = new_scr[...].T
    lane = lax.broadcasted_iota(jnp.int32, (src_ref.shape[1], LANES), 1)
    cur = pltpu.roll(src_ref[u, :, 0:LANES], shift, 1)
    for j in range(ncol):
        if j + 1 < ncol:
            nxt = pltpu.roll(src_ref[u, :, (j + 1) * LANES:(j + 2) * LANES], shift, 1)
        else:
            nxt = pltpu.roll(new_t, shift, 1)
        dst_ref[u, :, j * LANES:(j + 1) * LANES] = jnp.where(lane < shift, cur, nxt)
        cur = nxt


def _sample_attention(qkv, zb, kt_ref, vt_ref, u, knew, vnew, ob_ref, dil):
    T = qkv.shape[0]
    W = kt_ref.shape[-1]
    R = T * SW_HEADS
    ncol = W // LANES

    head_of_lane = lax.broadcasted_iota(jnp.int32, (SW_HEADS, SW_WIDTH), 1) >> (SW_HD.bit_length() - 1)
    head_mask = head_of_lane == lax.broadcasted_iota(jnp.int32, (SW_HEADS, SW_WIDTH), 0)
    qbd = jnp.concatenate(
        [jnp.where(head_mask, jnp.broadcast_to(qkv[t:t + 1, 0:SW_WIDTH], (SW_HEADS, SW_WIDTH)), 0.0)
         for t in range(T)], axis=0).astype(BF16)
    hm_rows = jnp.concatenate([head_mask] * T, axis=0)

    scale = SW_HD ** -0.5

    s_cols = [_dot(qbd, kt_ref[u, :, j * LANES:(j + 1) * LANES].astype(BF16)) * scale for j in range(ncol)]
    s_cols.append(_dot_nt(qbd, knew.astype(BF16)) * scale)
    s_all = jnp.concatenate(s_cols, axis=1)

    wtot = W + LANES
    j_idx = lax.broadcasted_iota(jnp.int32, (R, wtot), 1)
    t_idx = lax.broadcasted_iota(jnp.int32, (R, wtot), 0) >> (SW_HEADS.bit_length() - 1)
    dist = jnp.where(j_idx < W, W + t_idx - j_idx, t_idx - (j_idx - W))
    exists = (j_idx < W + T) & (dist >= 0)

    ms, ls, ps = [], [], []
    for win, d in dil:
        valid = exists & ((dist & (d - 1)) == 0) & (dist <= win)
        sb = jnp.where(valid, s_all, -jnp.inf)
        m = jnp.max(sb, axis=-1, keepdims=True)
        pe = jnp.exp(sb - m)
        ms.append(m)
        ls.append(jnp.sum(pe, axis=-1, keepdims=True))
        ps.append(pe.astype(BF16))
    p_all = jnp.concatenate(ps, axis=0)

    def finish():
        num = _dot(p_all[:, W:], vnew.astype(BF16))
        for j in range(ncol):
            cols = slice(j * LANES, (j + 1) * LANES)
            num = num + _dot_nt(p_all[:, cols], vt_ref[u, :, cols].astype(BF16))
        mx = functools.reduce(jnp.maximum, ms)
        wts = [jnp.exp(m - mx) for m in ms]
        numer = sum(wts[i] * num[i * R:(i + 1) * R] for i in range(len(dil)))
        denom = sum(wts[i] * ls[i] for i in range(len(dil)))
        o = jnp.where(hm_rows, numer / denom, 0.0)
        gate = _silu(zb)
        for t in range(T):
            row = jnp.sum(o[t * SW_HEADS:(t + 1) * SW_HEADS], axis=0, keepdims=True)
            ob_ref[u, t:t + 1, :] = row * gate[t:t + 1]

    return finish


def _out_kernel(x_ref, oa_ref, ob_ref, w_ref, fg_ref, y_ref):
    acc = _dot(oa_ref[0], w_ref[0:DN_WIDTH, :]) + _dot(ob_ref[0], w_ref[DN_WIDTH:, :])
    y = x_ref[0] + acc
    y_ref[0] = (y * lax.rsqrt(jnp.mean(y * y, axis=-1, keepdims=True) + NORM_EPS)) * fg_ref[...]


def _out(x, oa, ob, w, fg, *, tm, name):
    B, L, _ = x.shape
    row = lambda b, t: (b, t, 0)
    return pl.pallas_call(
        _out_kernel,
        grid=(B, L // tm),
        in_specs=[pl.BlockSpec((1, tm, D_MODEL), row), pl.BlockSpec((1, tm, DN_WIDTH), row),
                  pl.BlockSpec((1, tm, SW_WIDTH), row),
                  _const_spec((DN_WIDTH + SW_WIDTH, D_MODEL)), _const_spec((1, D_MODEL))],
        out_specs=pl.BlockSpec((1, tm, D_MODEL), row),
        out_shape=jax.ShapeDtypeStruct((B, L, D_MODEL), F32),
        compiler_params=pltpu.CompilerParams(
            dimension_semantics=("arbitrary", "arbitrary"), vmem_limit_bytes=VMEM_LIMIT),
        name=name,
    )(x, oa, ob, w, fg)


def _pad_lanes(v):
    return jnp.pad(v.astype(F32), (0, LANES - v.shape[0])).reshape(1, LANES)


def _tile(n, pref):
    return pref if n % pref == 0 else n


def _layers(xp, xs, conv_state, s0, k_win, v_win, wp, g_row, conv_w, alog_row, dtb_row, ng_row,
            w_out16, fg_row):
    B, T, _ = xs.shape
    W = k_win.shape[1]
    hist = CONV_W - 1
    group = SUBLANES
    assert T + hist <= group
    rows = xs.reshape(1, B * T, D_MODEL)
    raw, za, zb, ab, qkvb = _proj_sample(rows, g_row, wp, tm=_tile(B * T, SAMPLE_ROWS))
    raw = raw.reshape(B, T, CONV_CH)
    qkvb = qkvb.reshape(B, T, 3 * SW_WIDTH)
    front = lambda a: jnp.pad(a.reshape(B, T, -1), ((0, 0), (group - T, 0), (0, 0))).reshape(1, B * group, -1)
    ext = jnp.concatenate([jnp.zeros((B, group - T - hist, CONV_CH), F32), conv_state, raw], axis=1)
    qkvn, gb = _gdn_pre(ext.reshape(1, B * group, CONV_CH), front(ab), conv_w, alog_row, dtb_row,
                        group=group, t_new=T)
    kt = jnp.transpose(k_win, (0, 2, 3, 1)).reshape(B, SW_WIDTH, W)
    vt = jnp.transpose(v_win, (0, 2, 3, 1)).reshape(B, SW_WIDTH, W)

    Bp, L, _ = xp.shape
    keep = min(WIN_MAX, L)
    dils = tuple(d for _, d in DILATIONS)
    tm = _tile(L, PROJ_ROWS)
    p_qkvn, p_gb, p_za, p_zb, raw_last, k_last, v_last, qb1, qb4, qb16 = _proj_prompt(
        xp, g_row, wp, conv_w, alog_row, dtb_row, dils=dils, keep=keep, tm=tm)
    p_oa, p_s, kt_new, vt_new, ob = _gdn_prompt(p_qkvn, p_gb, p_za, ng_row, kt, vt, qkvb,
                                                zb.reshape(B, T, SW_WIDTH), C=DN_CHUNK, TL=_tile(L, GDN_TILE))
    steps = {win // d for win, d in DILATIONS}
    assert len(steps) == 1, "every dilation looks back the same number of class rows"
    blk = steps.pop()
    rt = _tile(L, ATTN_ROWS)
    qbs = dict(zip(dils, (qb1, qb4, qb16)))
    inner = tuple(d for d in dils if rt // d >= blk)
    ext = []
    for d in dils:
        if d not in inner:
            ext += _swa_wide(qbs[d], d=d, blk=blk)
    p_ob = _attn_prompt(tuple(qbs[d] for d in inner), ext, p_zb, dils=inner, blk=blk, rt=rt)
    yp = _out(xp, p_oa, p_ob, w_out16, fg_row, tm=tm, name="out_prompt")
    prompt = (yp, raw_last[:, HALO - hist:, :], p_s,
              k_last.reshape(Bp, keep, SW_HEADS, SW_HD), v_last.reshape(Bp, keep, SW_HEADS, SW_HD))

    oa, s_new = _gdn_sample(qkvn, gb, front(za), s0, ng_row, C=group, n=_tile(B * group, GDN_TILE))
    oa = oa.reshape(B, group, DN_WIDTH)[:, group - T:].reshape(1, B * T, DN_WIDTH)
    ys = _out(rows, oa, ob.reshape(1, B * T, SW_WIDTH).astype(BF16), w_out16, fg_row,
              tm=_tile(B * T, SAMPLE_ROWS), name="out_sample")
    sample = (ys.reshape(B, T, D_MODEL), jnp.concatenate([conv_state, raw], axis=1)[:, -hist:], s_new,
              jnp.transpose(kt_new.reshape(B, SW_HEADS, SW_HD, W), (0, 3, 1, 2)),
              jnp.transpose(vt_new.reshape(B, SW_HEADS, SW_HD, W), (0, 3, 1, 2)))
    return prompt, sample


def kernel(x_prompt, x_sample, state_conv, state_delta, cache_k_win, cache_v_win, norm_g, final_norm_g,
           w_in, conv_w, a_log, dt_bias, dn_norm_g, w_out):
    assert norm_g.shape[0] == 1, "single layer"
    w = w_in[0]
    o1 = CONV_CH + DN_WIDTH
    o3 = o1 + 2 * DN_HEADS
    o4 = o3 + 3 * SW_WIDTH
    wp = jnp.concatenate(
        [w[:, :o1], w[:, o3:], w[:, o1:o3], jnp.zeros((D_MODEL, LANES - 2 * DN_HEADS), w.dtype)],
        axis=1).astype(BF16)
    assert wp.shape[1] == W_COLS and o4 - o3 == 3 * SW_WIDTH
    g_row = norm_g[0].reshape(1, D_MODEL)
    fg_row = final_norm_g.reshape(1, D_MODEL)
    alog_row = _pad_lanes(a_log[0])
    dtb_row = _pad_lanes(dt_bias[0])
    ng_row = dn_norm_g[0].reshape(1, DN_DV)
    w_out16 = w_out[0].astype(BF16)
    cw = conv_w[0]
    common = (wp, g_row, cw, alog_row, dtb_row, ng_row, w_out16, fg_row)
    (yp, pc, ps, pk, pv), (ys, sc, ss, sk, sv) = _layers(
        x_prompt, x_sample, state_conv[0], state_delta[0], cache_k_win[0], cache_v_win[0], *common)
    return (yp, ys, pc[None], ps[None], pk[None], pv[None], sc[None], ss[None], sk[None], sv[None])
```

```python
import functools

import numpy as np
import jax
import jax.numpy as jnp
from jax import lax
from jax.experimental import pallas as pl
from jax.experimental.pallas import tpu as pltpu

F32 = jnp.float32
BF16 = jnp.bfloat16

D_MODEL = 1024
DN_HEADS = 4
DN_DK = 128
DN_DV = 128
QK_W = DN_HEADS * DN_DK
DN_WIDTH = DN_HEADS * DN_DV
CONV_W = 4
CONV_CH = 2 * QK_W + DN_WIDTH
DN_CHUNK = 64
SW_HEADS = 8
SW_HD = 64
SW_WIDTH = SW_HEADS * SW_HD
DILATIONS = ((128, 1), (512, 4), (2048, 16))
WIN_MAX = 2048
NORM_EPS = 1e-6

LANES = 128
SUBLANES = 8
VMEM_LIMIT = 56 * 1024 * 1024

C_RAW = 0
C_ZA = CONV_CH
C_QKVB = C_ZA + DN_WIDTH
C_ZB = C_QKVB + 3 * SW_WIDTH
C_AB = C_ZB + SW_WIDTH
W_COLS = C_AB + LANES
HALO = SUBLANES
GDN_TILE = DN_HEADS * DN_CHUNK
PROJ_ROWS = 512
SAMPLE_ROWS = 256
ATTN_ROWS = 1024


def _dot(a, b):
    return lax.dot_general(a, b, (((a.ndim - 1,), (0,)), ((), ())), preferred_element_type=F32)


def _dot_nt(a, b):
    return lax.dot_general(a, b, (((1,), (1,)), ((), ())), preferred_element_type=F32)


def _dot_tn(a, b):
    return lax.dot_general(a, b, (((0,), (0,)), ((), ())), preferred_element_type=F32)


def _bdot(a, b):
    return lax.dot_general(a, b, (((2,), (1,)), ((0,), (0,))), preferred_element_type=F32)


def _bdot_tn(a, b):
    return lax.dot_general(a, b, (((1,), (1,)), ((0,), (0,))), preferred_element_type=F32)


def _dot_exact_lhs(a16, x):
    hi = x.astype(BF16)
    r1 = x - hi.astype(F32)
    mid = r1.astype(BF16)
    lo = (r1 - mid.astype(F32)).astype(BF16)
    return _dot(a16, lo) + _dot(a16, mid) + _dot(a16, hi)


def _sigmoid(x):
    return 1.0 / (1.0 + jnp.exp(-x))


def _silu(x):
    return x * _sigmoid(x)


def _conv_norm_gates(ext_scr, cw_ref, alog_ref, dtb_ref, qkvn_ref, gb_ref, rows, live_from):
    ch = LANES if rows % LANES == 0 else rows
    cw = cw_ref[...]
    a_neg = -jnp.exp(alog_ref[...])
    dtb = dtb_ref[...]
    lane = lax.broadcasted_iota(jnp.int32, (ch, LANES), 1)

    def chunk(c, carry):
        r0 = c * ch
        out = pl.ds(r0, ch)
        for s in range(CONV_CH // LANES):
            sl = slice(s * LANES, (s + 1) * LANES)
            win = ext_scr[pl.ds(r0, ch + HALO), sl]
            y = cw[0:1, sl] * win[HALO - 3:HALO - 3 + ch]
            y = y + cw[1:2, sl] * win[HALO - 2:HALO - 2 + ch]
            y = y + cw[2:3, sl] * win[HALO - 1:HALO - 1 + ch]
            y = y + cw[3:4, sl] * win[HALO:HALO + ch]
            v = _silu(y)
            if s < DN_HEADS:
                v = v * lax.rsqrt(jnp.sum(v * v, -1, keepdims=True) + NORM_EPS) * (DN_DK ** -0.5)
            elif s < 2 * DN_HEADS:
                v = v * lax.rsqrt(jnp.sum(v * v, -1, keepdims=True) + NORM_EPS)
            qkvn_ref[0, out, sl] = v.astype(qkvn_ref.dtype)
        ab = gb_ref[0, out, :]
        xg = ab + dtb
        softplus = jnp.maximum(xg, 0.0) + jnp.log1p(jnp.exp(-jnp.abs(xg)))
        gb = jnp.where(lane < DN_HEADS, a_neg * softplus, _sigmoid(ab))
        if live_from is not None:
            group, first = live_from
            rowi = lax.broadcasted_iota(jnp.int32, (ch, LANES), 0) + r0
            gb = jnp.where((rowi & (group - 1)) >= first, gb, 0.0)
        gb_ref[0, out, :] = gb
        return carry

    for c in range(rows // ch):
        chunk(c, 0)


def _normed(x_ref, g_ref, h_scr):
    rows = x_ref.shape[1]
    ch = LANES if rows % LANES == 0 else rows
    g = g_ref[...]

    for c in range(rows // ch):
        rws = pl.ds(c * ch, ch)
        x = x_ref[0, rws, :]
        ms = jnp.mean(x * x, axis=-1, keepdims=True)
        h_scr[rws, :] = ((x * lax.rsqrt(ms + NORM_EPS)) * g).astype(BF16)
    return h_scr[...]


def _proj_sample_kernel(x_ref, g_ref, w_ref, raw_ref, za_ref, zb_ref, ab_ref, qb_ref, h_scr):
    h = _normed(x_ref, g_ref, h_scr)
    raw_ref[0] = _dot(h, w_ref[:, C_RAW:C_ZA])
    za_ref[0] = _dot(h, w_ref[:, C_ZA:C_QKVB])
    zb_ref[0] = _dot(h, w_ref[:, C_ZB:C_AB])
    ab_ref[0] = _dot(h, w_ref[:, C_AB:W_COLS])
    qb_ref[0] = _dot(h, w_ref[:, C_QKVB:C_ZB])


def _proj_prompt_kernel(x_ref, g_ref, w_ref, cw_ref, alog_ref, dtb_ref,
                        qkvn_ref, gb_ref, za_ref, zb_ref, rawlast_ref, klast_ref, vlast_ref, *rest, dils, tm):
    nd = len(dils)
    qb_refs = rest[:nd]
    cls_scrs = list(rest[nd:2 * nd - 1]) + [None]
    ext_scr, h_scr = rest[2 * nd - 1:]

    @pl.when(pl.program_id(1) == 0)
    def _():
        ext_scr[0:HALO, :] = jnp.zeros((HALO, CONV_CH), F32)

    _normed(x_ref, g_ref, h_scr)
    qkvb = _dot(h_scr[...], w_ref[:, C_QKVB:C_ZB])
    ext_scr[HALO:HALO + tm, :] = _dot(h_scr[...], w_ref[:, C_RAW:C_ZA])
    za_ref[0] = _dot(h_scr[...], w_ref[:, C_ZA:C_QKVB]).astype(za_ref.dtype)
    zb_ref[0] = _dot(h_scr[...], w_ref[:, C_ZB:C_AB]).astype(zb_ref.dtype)
    gb_ref[0] = _dot(h_scr[...], w_ref[:, C_AB:W_COLS])

    klast_ref[0] = qkvb[:, SW_WIDTH:2 * SW_WIDTH]
    vlast_ref[0] = qkvb[:, 2 * SW_WIDTH:]
    nslab = 3 * SW_WIDTH // LANES
    assert dils[0] == 1
    qb_refs[0][0, 0] = qkvb.astype(BF16)
    for j in range(nslab):
        cls_scrs[0][j] = qkvb[:, j * LANES:(j + 1) * LANES]
    for i in range(1, nd):
        d, prev = dils[i], dils[i - 1]
        ratio, n_d, n_prev = d // prev, tm // d, tm // prev
        assert d % prev == 0
        for r in range(d):
            for j in range(nslab):
                v = cls_scrs[i - 1][j, pl.ds((r % prev) * n_prev + r // prev, n_d, stride=ratio), :]
                qb_refs[i][0, r, :, j * LANES:(j + 1) * LANES] = v.astype(BF16)
                if cls_scrs[i] is not None:
                    cls_scrs[i][j, r * n_d:(r + 1) * n_d, :] = v

    _conv_norm_gates(ext_scr, cw_ref, alog_ref, dtb_ref, qkvn_ref, gb_ref, tm, None)
    last = ext_scr[tm:tm + HALO, :]
    rawlast_ref[0] = last
    ext_scr[0:HALO, :] = last


def _const_spec(shape):
    zeros = (0,) * len(shape)
    return pl.BlockSpec(shape, lambda *_: zeros)


def _proj_prompt(x, g, w, cw, alog_row, dtb_row, *, dils, keep, tm):
    B, L, _ = x.shape
    row = lambda b, t: (b, t, 0)
    first = (L - keep) // tm
    out_shape = [jax.ShapeDtypeStruct((B, L, CONV_CH), F32),
                 jax.ShapeDtypeStruct((B, L, LANES), F32),
                 jax.ShapeDtypeStruct((B, L, DN_WIDTH), BF16),
                 jax.ShapeDtypeStruct((B, L, SW_WIDTH), BF16),
                 jax.ShapeDtypeStruct((B, HALO, CONV_CH), F32),
                 jax.ShapeDtypeStruct((B, keep, SW_WIDTH), F32),
                 jax.ShapeDtypeStruct((B, keep, SW_WIDTH), F32)]
    out_specs = [pl.BlockSpec((1, tm, CONV_CH), row),
                 pl.BlockSpec((1, tm, LANES), row),
                 pl.BlockSpec((1, tm, DN_WIDTH), row),
                 pl.BlockSpec((1, tm, SW_WIDTH), row),
                 pl.BlockSpec((1, HALO, CONV_CH), lambda b, t: (b, 0, 0)),
                 pl.BlockSpec((1, tm, SW_WIDTH), lambda b, t: (b, jnp.maximum(t - first, 0), 0)),
                 pl.BlockSpec((1, tm, SW_WIDTH), lambda b, t: (b, jnp.maximum(t - first, 0), 0))]
    for d in dils:
        out_shape.append(jax.ShapeDtypeStruct((B, d, L // d, 3 * SW_WIDTH), BF16))
        out_specs.append(pl.BlockSpec((1, d, tm // d, 3 * SW_WIDTH), lambda b, t: (b, 0, t, 0)))
    return pl.pallas_call(
        functools.partial(_proj_prompt_kernel, dils=dils, tm=tm),
        grid=(B, L // tm),
        in_specs=[pl.BlockSpec((1, tm, D_MODEL), row),
                  _const_spec((1, D_MODEL)),
                  pl.BlockSpec((D_MODEL, W_COLS), lambda b, t: (0, 0), pipeline_mode=pl.Buffered(1)),
                  _const_spec((CONV_W, CONV_CH)),
                  _const_spec((1, LANES)),
                  _const_spec((1, LANES))],
        out_specs=out_specs,
        out_shape=out_shape,
        scratch_shapes=[pltpu.VMEM((3 * SW_WIDTH // LANES, tm, LANES), F32)] * (len(dils) - 1)
        + [pltpu.VMEM((tm + HALO, CONV_CH), F32), pltpu.VMEM((tm, D_MODEL), BF16)],
        compiler_params=pltpu.CompilerParams(
            dimension_semantics=("arbitrary", "arbitrary"), vmem_limit_bytes=VMEM_LIMIT),
        name="proj_prompt",
    )(x, g, w, cw, alog_row, dtb_row)


def _proj_sample(x, g, w, *, tm):
    B, L, _ = x.shape
    row = lambda b, t: (b, t, 0)
    widths = (CONV_CH, DN_WIDTH, SW_WIDTH, LANES, 3 * SW_WIDTH)
    return pl.pallas_call(
        _proj_sample_kernel,
        grid=(B, L // tm),
        in_specs=[pl.BlockSpec((1, tm, D_MODEL), row), _const_spec((1, D_MODEL)),
                  _const_spec((D_MODEL, W_COLS))],
        out_specs=[pl.BlockSpec((1, tm, c), row) for c in widths],
        out_shape=[jax.ShapeDtypeStruct((B, L, c), F32) for c in widths],
        scratch_shapes=[pltpu.VMEM((tm, D_MODEL), BF16)],
        compiler_params=pltpu.CompilerParams(
            dimension_semantics=("arbitrary", "arbitrary"), vmem_limit_bytes=VMEM_LIMIT),
        name="proj_sample",
    )(x, g, w)


def _gdn_pre_kernel(ext_ref, ab_ref, cw_ref, alog_ref, dtb_ref, qkvn_ref, gb_ref, ext_scr, *, rows, group, t_new):
    ext_scr[0:HALO, :] = jnp.zeros((HALO, CONV_CH), F32)
    ext_scr[HALO:HALO + rows, :] = ext_ref[0]
    gb_ref[0] = ab_ref[0]
    _conv_norm_gates(ext_scr, cw_ref, alog_ref, dtb_ref, qkvn_ref, gb_ref, rows, (group, group - t_new))


def _gdn_pre(ext, ab, cw, alog_row, dtb_row, *, group, t_new):
    _, rows, _ = ext.shape
    return pl.pallas_call(
        functools.partial(_gdn_pre_kernel, rows=rows, group=group, t_new=t_new),
        grid=(1,),
        in_specs=[_const_spec((1, rows, CONV_CH)), _const_spec((1, rows, LANES)),
                  _const_spec((CONV_W, CONV_CH)), _const_spec((1, LANES)), _const_spec((1, LANES))],
        out_specs=[_const_spec((1, rows, CONV_CH)), _const_spec((1, rows, LANES))],
        out_shape=[jax.ShapeDtypeStruct((1, rows, CONV_CH), F32), jax.ShapeDtypeStruct((1, rows, LANES), F32)],
        scratch_shapes=[pltpu.VMEM((rows + HALO, CONV_CH), F32)],
        compiler_params=pltpu.CompilerParams(
            dimension_semantics=("arbitrary",), vmem_limit_bytes=VMEM_LIMIT),
        name="gdn_pre_sample",
    )(ext, ab, cw, alog_row, dtb_row)


def _block_masks(n, c):
    r = np.arange(n)[:, None]
    cc = np.arange(n)[None, :]
    x = r ^ cc
    same = x < c
    incl = same & (cc <= r)
    levels = []
    k = 1
    while (2 << k) <= c:
        levels.append((x >> k) == 1)
        k += 1
    masks = np.stack([r == cc, incl, (x >> 1) == 0]).astype(np.float32)
    lmask = np.concatenate([incl, same] + levels, axis=0).astype(np.float32)
    return jnp.asarray(masks), jnp.asarray(lmask, dtype=BF16)


def _gdn_tiles(tiles, masks_ref, lmask_ref):
    n = tiles[0][0].shape[0]
    nt = len(tiles)
    nlev = lmask_ref.shape[0] // n - 2
    eye = masks_ref[0]
    incl = masks_ref[1]
    g_b = [jnp.broadcast_to(t[3], (n, LANES)) for t in tiles]
    beta_b = [jnp.broadcast_to(t[4], (n, LANES)) for t in tiles]
    cs = _dot_exact_lhs(lmask_ref[0:2 * n, :], jnp.concatenate(g_b, axis=1))
    gc_b = [cs[:n, i * LANES:(i + 1) * LANES] for i in range(nt)]
    gl_b = [cs[n:, i * LANES:(i + 1) * LANES] for i in range(nt)]
    kb = [tiles[i][1] * beta_b[i] for i in range(nt)]
    k16 = [t[1].astype(BF16) for t in tiles]
    kk = [_dot_nt(kb[i].astype(BF16), k16[i]) for i in range(nt)]
    qk = [_dot_nt(tiles[i][0].astype(BF16), k16[i]) for i in range(nt)]
    a16, attn, t = [], [], []
    for i in range(nt):
        gc_sq = jnp.concatenate([gc_b[i]] * (n // LANES), axis=1)
        gc_row = jnp.sum(gc_sq * eye, axis=0, keepdims=True)
        dec = jnp.exp(jnp.minimum(gc_sq - gc_row, 0.0)) * incl
        a = kk[i] * (dec - eye)
        t.append(eye - a * masks_ref[2])
        a16.append(a.astype(BF16))
        attn.append((qk[i] * dec).astype(BF16))
    for lv in range(nlev):
        t16 = [x.astype(BF16) for x in t]
        m16 = lmask_ref[(2 + lv) * n:(3 + lv) * n, :]
        p = [_dot(t16[i], a16[i] * m16) for i in range(nt)]
        t = [t[i] - _dot(p[i].astype(BF16), t16[i]) for i in range(nt)]
    e_gc = [jnp.exp(x) for x in gc_b]
    sol = [_dot(t[i].astype(BF16),
                jnp.concatenate([tiles[i][2] * beta_b[i], kb[i] * e_gc[i]], axis=1).astype(BF16))
           for i in range(nt)]
    return [(sol[i][:, :DN_DV], sol[i][:, DN_DV:], tiles[i][0] * e_gc[i],
             tiles[i][1] * jnp.exp(gl_b[i] - gc_b[i]), attn[i], gl_b[i]) for i in range(nt)]


def _out_norm(o, ng, za):
    return ((o * lax.rsqrt(jnp.mean(o * o, -1, keepdims=True) + NORM_EPS)) * ng) * _silu(za.astype(F32))


def _gdn_prompt_kernel(qkvn_ref, gb_ref, za_ref, ng_ref, masks_ref, lmask_ref, kt_ref, vt_ref, new_ref,
                       zbs_ref, o_ref, s_out_ref, kto_ref, vto_ref, obs_ref, s_scr, knew_scr, vnew_scr,
                       *, C, TL, G):
    H = DN_HEADS

    @pl.when(pl.program_id(1) == 0)
    def _():
        s_scr[...] = jnp.zeros_like(s_scr)

    @pl.when((pl.program_id(0) == 0) & (pl.program_id(1) == 0))
    def _():
        knew_scr[...] = jnp.zeros_like(knew_scr)
        vnew_scr[...] = jnp.zeros_like(vnew_scr)

    finish_sample = []
    for u in range(kt_ref.shape[0]):
        _shift_window(kt_ref, u, new_ref[u, :, SW_WIDTH:2 * SW_WIDTH], knew_scr, kto_ref)
        _shift_window(vt_ref, u, new_ref[u, :, 2 * SW_WIDTH:], vnew_scr, vto_ref)
        finish_sample.append(_sample_attention(new_ref[u], zbs_ref[u], kt_ref, vt_ref, u, knew_scr[...],
                                               vnew_scr[...], obs_ref, DILATIONS))

    ng = ng_ref[...]

    def group(gi, carry):
        base = gi * (G * C) if isinstance(gi, int) else pl.multiple_of(gi * (G * C), G * C)
        rows = [pl.ds(base + i * C, C) for i in range(G)]

        def stack(ref, rws, off):
            return jnp.concatenate(
                [ref[0, rws, off + h * LANES:off + (h + 1) * LANES] for h in range(H)], axis=0)

        tiles, zas = [], []
        for rws in rows:
            gbc = gb_ref[0, rws, :]
            g_col = jnp.concatenate([gbc[:, h:h + 1] for h in range(H)], axis=0)
            beta_col = jnp.concatenate([gbc[:, H + h:H + h + 1] for h in range(H)], axis=0)
            zas.append(stack(za_ref, rws, 0))
            tiles.append((stack(qkvn_ref, rws, 0).astype(F32), stack(qkvn_ref, rws, QK_W).astype(F32),
                          stack(qkvn_ref, rws, 2 * QK_W).astype(F32), g_col, beta_col))
        tiles = _gdn_tiles(tiles, masks_ref, lmask_ref)
        s = [s_scr[h] for h in range(H)]
        outs = []
        for (u, w, q_dec, k_dec, attn, gl_b), za in zip(tiles, zas):
            ws, qs = [], []
            for h in range(H):
                blk = slice(h * C, (h + 1) * C)
                r = _dot(jnp.concatenate([w[blk], q_dec[blk]], axis=0).astype(BF16), s[h].astype(BF16))
                ws.append(r[:C])
                qs.append(r[C:])
            v_new = u - jnp.concatenate(ws, axis=0)
            vn16 = v_new.astype(BF16)
            o = jnp.concatenate(qs, axis=0) + _dot(attn.astype(BF16), vn16)
            for h in range(H):
                blk = slice(h * C, (h + 1) * C)
                gl_s = jnp.broadcast_to(jnp.exp(gl_b[h * C:h * C + 1, :]), (DN_DK, DN_DV))
                s[h] = s[h] * gl_s + _dot_tn(k_dec[blk].astype(BF16), vn16[blk])
            outs.append(_out_norm(o, ng, za))
        for rws, out in zip(rows, outs):
            for h in range(H):
                o_ref[0, rws, h * DN_DV:(h + 1) * DN_DV] = out[h * C:(h + 1) * C].astype(o_ref.dtype)
        for h in range(H):
            s_scr[h] = s[h]
        return carry

    for gi in range(TL // (G * C)):
        group(gi, 0)
    for finish in finish_sample:
        finish()
    s_out_ref[0] = s_scr[...]


def _gdn_prompt(qkvn, gb, za, ng_row, kt, vt, new_rows, zbs, *, C, TL):
    B, L, _ = qkvn.shape
    n = DN_HEADS * C
    nt = L // TL
    Bs, T, _ = new_rows.shape
    W = kt.shape[-1]
    assert Bs % (B * nt) == 0
    per = Bs // (B * nt)
    masks, lmask = _block_masks(n, C)
    row = lambda b, t: (b, t, 0)
    share = lambda b, t: (b * nt + t, 0, 0)
    window = pl.BlockSpec((per, SW_WIDTH, W), share)
    return pl.pallas_call(
        functools.partial(_gdn_prompt_kernel, C=C, TL=TL, G=4 if TL % (4 * C) == 0 else 1),
        grid=(B, nt),
        in_specs=[pl.BlockSpec((1, TL, CONV_CH), row),
                  pl.BlockSpec((1, TL, LANES), row),
                  pl.BlockSpec((1, TL, DN_WIDTH), row),
                  _const_spec((1, DN_DV)), _const_spec(masks.shape), _const_spec(lmask.shape),
                  window, window, pl.BlockSpec((per, T, 3 * SW_WIDTH), share),
                  pl.BlockSpec((per, T, SW_WIDTH), share)],
        out_specs=[pl.BlockSpec((1, TL, DN_WIDTH), row),
                   pl.BlockSpec((1, DN_HEADS, DN_DK, DN_DV), lambda b, t: (b, 0, 0, 0)),
                   window, window, pl.BlockSpec((per, T, SW_WIDTH), share)],
        out_shape=[jax.ShapeDtypeStruct((B, L, DN_WIDTH), BF16),
                   jax.ShapeDtypeStruct((B, DN_HEADS, DN_DK, DN_DV), F32),
                   jax.ShapeDtypeStruct(kt.shape, F32), jax.ShapeDtypeStruct(vt.shape, F32),
                   jax.ShapeDtypeStruct((Bs, T, SW_WIDTH), F32)],
        scratch_shapes=[pltpu.VMEM((DN_HEADS, DN_DK, DN_DV), F32),
                        pltpu.VMEM((LANES, SW_WIDTH), F32), pltpu.VMEM((LANES, SW_WIDTH), F32)],
        compiler_params=pltpu.CompilerParams(
            dimension_semantics=("arbitrary", "arbitrary"), vmem_limit_bytes=VMEM_LIMIT),
        name="gdn_prompt",
    )(qkvn, gb, za, ng_row, masks, lmask, kt, vt, new_rows, zbs)


def _gdn_sample_kernel(qkvn_ref, gb_ref, za_ref, s0_ref, ng_ref, masks_ref, lmask_ref, o_ref, s_out_ref, *, C):
    n = qkvn_ref.shape[1]
    nb = n // C
    ng = ng_ref[...]
    split = lambda a: a.reshape(nb, C, a.shape[-1])
    tiles = _gdn_tiles(
        [(qkvn_ref[0, :, h * LANES:(h + 1) * LANES],
          qkvn_ref[0, :, QK_W + h * LANES:QK_W + (h + 1) * LANES],
          qkvn_ref[0, :, 2 * QK_W + h * LANES:2 * QK_W + (h + 1) * LANES],
          gb_ref[0, :, h:h + 1], gb_ref[0, :, DN_HEADS + h:DN_HEADS + h + 1]) for h in range(DN_HEADS)],
        masks_ref, lmask_ref)
    for h in range(DN_HEADS):
        sl = slice(h * LANES, (h + 1) * LANES)
        u, w, q_dec, k_dec, attn, gl_b = tiles[h]
        s0 = s0_ref[:, h]
        r = _bdot(jnp.concatenate([split(w), split(q_dec)], axis=1).astype(BF16), s0.astype(BF16))
        v_new = split(u) - r[:, :C]
        vn16 = v_new.astype(BF16)
        o = r[:, C:].reshape(n, DN_DV) + _dot(attn.astype(BF16), vn16.reshape(n, DN_DV))
        gl = split(jnp.exp(gl_b))[:, 0:1, :]
        s_out_ref[:, h] = s0 * gl + _bdot_tn(split(k_dec).astype(BF16), vn16)
        o_ref[0, :, sl] = _out_norm(o, ng, za_ref[0, :, sl]).astype(o_ref.dtype)


def _gdn_sample(qkvn, gb, za, s0, ng_row, *, C, n):
    _, rows, _ = qkvn.shape
    nb = n // C
    masks, lmask = _block_masks(n, C)
    row = lambda i: (0, i, 0)
    state = pl.BlockSpec((nb, DN_HEADS, DN_DK, DN_DV), lambda i: (i, 0, 0, 0))
    return pl.pallas_call(
        functools.partial(_gdn_sample_kernel, C=C),
        grid=(rows // n,),
        in_specs=[pl.BlockSpec((1, n, CONV_CH), row), pl.BlockSpec((1, n, LANES), row),
                  pl.BlockSpec((1, n, DN_WIDTH), row), state,
                  _const_spec((1, DN_DV)), _const_spec(masks.shape), _const_spec(lmask.shape)],
        out_specs=[pl.BlockSpec((1, n, DN_WIDTH), row), state],
        out_shape=[jax.ShapeDtypeStruct((1, rows, DN_WIDTH), BF16), jax.ShapeDtypeStruct(s0.shape, F32)],
        compiler_params=pltpu.CompilerParams(
            dimension_semantics=("arbitrary",), vmem_limit_bytes=VMEM_LIMIT),
        name="gdn_sample",
    )(qkvn, gb, za, s0, ng_row, masks, lmask)


def _swa_blocks(items):
    mq = items[0][0].shape[0]
    npair = SW_WIDTH // LANES
    nh = 2 * npair
    lo_half = lax.broadcasted_iota(jnp.int32, (mq, LANES), 1) < SW_HD
    scale = SW_HD ** -0.5
    half_masks = (jnp.where(lo_half, scale, 0.0).astype(BF16), jnp.where(lo_half, 0.0, scale).astype(BF16))
    scores = []
    for q, kk, _, _ in items:
        for p in range(npair):
            sl = slice(p * LANES, (p + 1) * LANES)
            for half in range(2):
                scores.append(_dot_nt(q[:, sl] * half_masks[half], kk[:, sl]))
    probs, inv_l, lses = [], [], []
    for i, s in enumerate(scores):
        s = jnp.where(items[i // nh][3], s, -jnp.inf)
        m = jnp.max(s, axis=-1, keepdims=True)
        pe = jnp.exp(s - m)
        l = jnp.sum(pe, axis=-1, keepdims=True)
        probs.append(pe.astype(BF16))
        inv_l.append(1.0 / l)
        lses.append(jnp.broadcast_to(m + jnp.log(l), (mq, LANES)))
    pvs = [_dot(probs[i], items[i // nh][2][:, ((i % nh) // 2) * LANES:((i % nh) // 2 + 1) * LANES])
           for i in range(len(scores))]
    out = []
    for b in range(len(items)):
        e = [b * nh + 2 * p for p in range(npair)]
        out.append([(jnp.where(lo_half, pvs[i] * inv_l[i], pvs[i + 1] * inv_l[i + 1]),
                     jnp.where(lo_half, lses[i], lses[i + 1])) for i in e])
    return out


def _swa_geometry(rt, d, blk):
    qn = rt // d
    sb = min(blk, qn)
    return qn, sb, blk // sb


def _swa_wide_kernel(q_ref, kp_ref, kc_ref, vp_ref, vc_ref, o_ref, lse_ref, *, d, blk, gs):
    qi = lax.broadcasted_iota(jnp.int32, (blk, 2 * blk), 0)
    kj = lax.broadcasted_iota(jnp.int32, (blk, 2 * blk), 1)
    band = (kj >= qi) & (kj <= qi + blk) & ((kj >= blk) | (pl.program_id(1) > 0))
    npair = SW_WIDTH // LANES

    def group(g, carry):
        classes = [g * gs + i for i in range(gs)]
        items = [(q_ref[0, r], jnp.concatenate([kp_ref[0, r], kc_ref[0, r]], axis=0),
                  jnp.concatenate([vp_ref[0, r], vc_ref[0, r]], axis=0), band) for r in classes]
        for r, res in zip(classes, _swa_blocks(items)):
            rows = pl.ds(r, blk, stride=d)
            for p in range(npair):
                o_ref[0, p, rows, :] = res[p][0]
                lse_ref[0, p, rows, :] = res[p][1]
        return carry

    lax.fori_loop(0, d // gs, group, 0)


def _swa_wide(qb, *, d, blk):
    B, _, n, _ = qb.shape
    cur = lambda c: (lambda b, i: (b, 0, i, c))
    prev = lambda c: (lambda b, i: (b, 0, jnp.maximum(i - 1, 0), c))
    spec = lambda f: pl.BlockSpec((1, d, blk, SW_WIDTH), f)
    npair = SW_WIDTH // LANES
    out_spec = pl.BlockSpec((1, npair, blk * d, LANES), lambda b, i: (b, 0, i, 0))
    return pl.pallas_call(
        functools.partial(_swa_wide_kernel, d=d, blk=blk, gs=2),
        grid=(B, n // blk),
        in_specs=[spec(cur(0)), spec(prev(1)), spec(cur(1)), spec(prev(2)), spec(cur(2))],
        out_specs=[out_spec, out_spec],
        out_shape=[jax.ShapeDtypeStruct((B, npair, n * d, LANES), F32)] * 2,
        compiler_params=pltpu.CompilerParams(
            dimension_semantics=("arbitrary", "arbitrary"), vmem_limit_bytes=VMEM_LIMIT),
        name="swa_d%d" % d,
    )(qb, qb, qb, qb, qb)


def _attn_kernel(*refs, dils, n_ext, rt, blk):
    nbr = len(dils)
    zb_ref, ob_ref, o_scr, lse_scr = refs[-4:]
    ext = refs[-4 - 2 * n_ext:-4]
    t = pl.program_id(1)
    npair = SW_WIDTH // LANES
    idx = 0
    for bi, d in enumerate(dils):
        qn, sb, npiece = _swa_geometry(rt, d, blk)
        q_ref, k_ref, v_ref = refs[idx:idx + 3]
        kp = refs[idx + 3:idx + 3 + npiece]
        vp = refs[idx + 3 + npiece:idx + 3 + 2 * npiece]
        idx += 3 + 2 * npiece
        nk = blk + sb
        qi = lax.broadcasted_iota(jnp.int32, (sb, nk), 0)
        kj = lax.broadcasted_iota(jnp.int32, (sb, nk), 1)
        band = (kj >= qi) & (kj <= qi + blk)
        band0 = band & (kj >= blk - t * qn)

        nsub = qn // sb

        def first(r, q_ref=q_ref, k_ref=k_ref, v_ref=v_ref, kp=kp, vp=vp, sb=sb, band0=band0):
            kk = jnp.concatenate([x[0, r] for x in kp] + [k_ref[0, r, 0:sb, :]], axis=0)
            vv = jnp.concatenate([x[0, r] for x in vp] + [v_ref[0, r, 0:sb, :]], axis=0)
            return r, 0, (q_ref[0, r, 0:sb, :], kk, vv, band0)

        def later(r, j, q_ref=q_ref, k_ref=k_ref, v_ref=v_ref, sb=sb, band=band):
            aligned = (lambda x: x) if isinstance(j, int) else (lambda x: pl.multiple_of(x, sb))
            keys = pl.ds(aligned((j - 1) * sb), 2 * sb)
            return r, j, (q_ref[0, r, pl.ds(aligned(j * sb), sb), :],
                          k_ref[0, r, keys, :], v_ref[0, r, keys, :], band)

        def run(blocks, bi=bi, d=d, sb=sb):
            for (r, j, _), res in zip(blocks, _swa_blocks([b[2] for b in blocks])):
                start = j * (sb * d) + r
                rows = pl.ds(start, sb, stride=d) if d > 1 else pl.ds(start, sb)
                for p in range(npair):
                    o_scr[bi, p, rows, :] = res[p][0]
                    lse_scr[bi, p, rows, :] = res[p][1]

        if nsub == 1:
            gs = 4
            assert d % gs == 0
            lax.fori_loop(0, d // gs, lambda g, c, run=run, first=first:
                          (run([first(gs * g + i) for i in range(gs)]), c)[1], 0)
        else:
            assert nsub % 2 == 0

            def do_class(r, run=run, first=first, later=later, nsub=nsub):
                run([first(r), later(r, 1)])
                if nsub > 2:
                    lax.fori_loop(1, nsub // 2, lambda g, c:
                                  (run([later(r, 2 * g), later(r, 2 * g + 1)]), c)[1], 0)

            if d == 1:
                do_class(0)
            else:
                lax.fori_loop(0, d, lambda r, c, f=do_class: (f(r), c)[1], 0)

    def merge(c, carry):
        rows = pl.ds(pl.multiple_of(c * blk, blk), blk)
        zb = zb_ref[0, rows, :].astype(F32)
        for p in range(npair):
            lses = [lse_scr[i, p, rows, :] for i in range(nbr)]
            lses += [ext[2 * i + 1][0, p, rows, :] for i in range(n_ext)]
            outs = [o_scr[i, p, rows, :] for i in range(nbr)] + [ext[2 * i][0, p, rows, :] for i in range(n_ext)]
            mx = functools.reduce(jnp.maximum, lses)
            wts = [jnp.exp(l - mx) for l in lses]
            numer = sum(w * o for w, o in zip(wts, outs))
            sl = slice(p * LANES, (p + 1) * LANES)
            ob_ref[0, rows, sl] = ((numer / sum(wts)) * _silu(zb[:, sl])).astype(ob_ref.dtype)
        return carry

    lax.fori_loop(0, rt // blk, merge, 0)


def _attn_prompt(qbs, ext, zb, *, dils, blk, rt):
    B, L, _ = zb.shape
    args, specs = [], []
    for qb, d in zip(qbs, dils):
        qn, sb, npiece = _swa_geometry(rt, d, blk)
        for c in range(3):
            specs.append(pl.BlockSpec((1, d, qn, SW_WIDTH), lambda b, t, c=c: (b, 0, t, c)))
            args.append(qb)
        for c in (1, 2):
            for pi in range(npiece):
                back, per = npiece - pi, qn // sb
                specs.append(pl.BlockSpec(
                    (1, d, sb, SW_WIDTH),
                    lambda b, t, c=c, back=back, per=per: (b, 0, jnp.maximum(t * per - back, 0), c)))
                args.append(qb)
    npair = SW_WIDTH // LANES
    for a in ext:
        specs.append(pl.BlockSpec((1, npair, rt, LANES), lambda b, t: (b, 0, t, 0)))
        args.append(a)
    specs.append(pl.BlockSpec((1, rt, SW_WIDTH), lambda b, t: (b, t, 0)))
    args.append(zb)
    return pl.pallas_call(
        functools.partial(_attn_kernel, dils=dils, n_ext=len(ext) // 2, rt=rt, blk=blk),
        grid=(B, L // rt),
        in_specs=specs,
        out_specs=pl.BlockSpec((1, rt, SW_WIDTH), lambda b, t: (b, t, 0)),
        out_shape=jax.ShapeDtypeStruct((B, L, SW_WIDTH), BF16),
        scratch_shapes=[pltpu.VMEM((len(dils), npair, rt, LANES), F32),
                        pltpu.VMEM((len(dils), npair, rt, LANES), F32)],
        compiler_params=pltpu.CompilerParams(
            dimension_semantics=("arbitrary", "arbitrary"), vmem_limit_bytes=VMEM_LIMIT),
        name="attn_prompt",
    )(*args)


def _shift_window(src_ref, u, new_rows, new_scr, dst_ref):
    T = new_rows.shape[0]
    ncol = src_ref.shape[-1] // LANES
    shift = LANES - T
    new_scr[0:T, :] = new_rows
    new_t = new_scr[...].T
    lane = lax.broadcasted_iota(jnp.int32, (src_ref.shape[1], LANES), 1)
    cur = pltpu.roll(src_ref[u, :, 0:LANES], shift, 1)
    for j in range(ncol):
        if j + 1 < ncol:
            nxt = pltpu.roll(src_ref[u, :, (j + 1) * LANES:(j + 2) * LANES], shift, 1)
        else:
            nxt = pltpu.roll(new_t, shift, 1)
        dst_ref[u, :, j * LANES:(j + 1) * LANES] = jnp.where(lane < shift, cur, nxt)
        cur = nxt


def _sample_attention(qkv, zb, kt_ref, vt_ref, u, knew, vnew, ob_ref, dil):
    T = qkv.shape[0]
    W = kt_ref.shape[-1]
    R = T * SW_HEADS
    ncol = W // LANES

    head_of_lane = lax.broadcasted_iota(jnp.int32, (SW_HEADS, SW_WIDTH), 1) >> (SW_HD.bit_length() - 1)
    head_mask = head_of_lane == lax.broadcasted_iota(jnp.int32, (SW_HEADS, SW_WIDTH), 0)
    qbd = jnp.concatenate(
        [jnp.where(head_mask, jnp.broadcast_to(qkv[t:t + 1, 0:SW_WIDTH], (SW_HEADS, SW_WIDTH)), 0.0)
         for t in range(T)], axis=0).astype(BF16)
    hm_rows = jnp.concatenate([head_mask] * T, axis=0)

    scale = SW_HD ** -0.5

    s_cols = [_dot(qbd, kt_ref[u, :, j * LANES:(j + 1) * LANES].astype(BF16)) * scale for j in range(ncol)]
    s_cols.append(_dot_nt(qbd, knew.astype(BF16)) * scale)
    s_all = jnp.concatenate(s_cols, axis=1)

    wtot = W + LANES
    j_idx = lax.broadcasted_iota(jnp.int32, (R, wtot), 1)
    t_idx = lax.broadcasted_iota(jnp.int32, (R, wtot), 0) >> (SW_HEADS.bit_length() - 1)
    dist = jnp.where(j_idx < W, W + t_idx - j_idx, t_idx - (j_idx - W))
    exists = (j_idx < W + T) & (dist >= 0)

    ms, ls, ps = [], [], []
    for win, d in dil:
        valid = exists & ((dist & (d - 1)) == 0) & (dist <= win)
        sb = jnp.where(valid, s_all, -jnp.inf)
        m = jnp.max(sb, axis=-1, keepdims=True)
        pe = jnp.exp(sb - m)
        ms.append(m)
        ls.append(jnp.sum(pe, axis=-1, keepdims=True))
        ps.append(pe.astype(BF16))
    p_all = jnp.concatenate(ps, axis=0)

    def finish():
        num = _dot(p_all[:, W:], vnew.astype(BF16))
        for j in range(ncol):
            cols = slice(j * LANES, (j + 1) * LANES)
            num = num + _dot_nt(p_all[:, cols], vt_ref[u, :, cols].astype(BF16))
        mx = functools.reduce(jnp.maximum, ms)
        wts = [jnp.exp(m - mx) for m in ms]
        numer = sum(wts[i] * num[i * R:(i + 1) * R] for i in range(len(dil)))
        denom = sum(wts[i] * ls[i] for i in range(len(dil)))
        o = jnp.where(hm_rows, numer / denom, 0.0)
        gate = _silu(zb)
        for t in range(T):
            row = jnp.sum(o[t * SW_HEADS:(t + 1) * SW_HEADS], axis=0, keepdims=True)
            ob_ref[u, t:t + 1, :] = row * gate[t:t + 1]

    return finish


def _out_kernel(x_ref, oa_ref, ob_ref, w_ref, fg_ref, y_ref):
    acc = _dot(oa_ref[0], w_ref[0:DN_WIDTH, :]) + _dot(ob_ref[0], w_ref[DN_WIDTH:, :])
    y = x_ref[0] + acc
    y_ref[0] = (y * lax.rsqrt(jnp.mean(y * y, axis=-1, keepdims=True) + NORM_EPS)) * fg_ref[...]


def _out(x, oa, ob, w, fg, *, tm, name):
    B, L, _ = x.shape
    row = lambda b, t: (b, t, 0)
    return pl.pallas_call(
        _out_kernel,
        grid=(B, L // tm),
        in_specs=[pl.BlockSpec((1, tm, D_MODEL), row), pl.BlockSpec((1, tm, DN_WIDTH), row),
                  pl.BlockSpec((1, tm, SW_WIDTH), row),
                  _const_spec((DN_WIDTH + SW_WIDTH, D_MODEL)), _const_spec((1, D_MODEL))],
        out_specs=pl.BlockSpec((1, tm, D_MODEL), row),
        out_shape=jax.ShapeDtypeStruct((B, L, D_MODEL), F32),
        compiler_params=pltpu.CompilerParams(
            dimension_semantics=("arbitrary", "arbitrary"), vmem_limit_bytes=VMEM_LIMIT),
        name=name,
    )(x, oa, ob, w, fg)


def _pad_lanes(v):
    return jnp.pad(v.astype(F32), (0, LANES - v.shape[0])).reshape(1, LANES)


def _tile(n, pref):
    return pref if n % pref == 0 else n


def _layers(xp, xs, conv_state, s0, k_win, v_win, wp, g_row, conv_w, alog_row, dtb_row, ng_row,
            w_out16, fg_row):
    B, T, _ = xs.shape
    W = k_win.shape[1]
    hist = CONV_W - 1
    group = SUBLANES
    assert T + hist <= group
    rows = xs.reshape(1, B * T, D_MODEL)
    raw, za, zb, ab, qkvb = _proj_sample(rows, g_row, wp, tm=_tile(B * T, SAMPLE_ROWS))
    raw = raw.reshape(B, T, CONV_CH)
    qkvb = qkvb.reshape(B, T, 3 * SW_WIDTH)
    front = lambda a: jnp.pad(a.reshape(B, T, -1), ((0, 0), (group - T, 0), (0, 0))).reshape(1, B * group, -1)
    ext = jnp.concatenate([jnp.zeros((B, group - T - hist, CONV_CH), F32), conv_state, raw], axis=1)
    qkvn, gb = _gdn_pre(ext.reshape(1, B * group, CONV_CH), front(ab), conv_w, alog_row, dtb_row,
                        group=group, t_new=T)
    kt = jnp.transpose(k_win, (0, 2, 3, 1)).reshape(B, SW_WIDTH, W)
    vt = jnp.transpose(v_win, (0, 2, 3, 1)).reshape(B, SW_WIDTH, W)

    Bp, L, _ = xp.shape
    keep = min(WIN_MAX, L)
    dils = tuple(d for _, d in DILATIONS)
    tm = _tile(L, PROJ_ROWS)
    p_qkvn, p_gb, p_za, p_zb, raw_last, k_last, v_last, qb1, qb4, qb16 = _proj_prompt(
        xp, g_row, wp, conv_w, alog_row, dtb_row, dils=dils, keep=keep, tm=tm)
    p_oa, p_s, kt_new, vt_new, ob = _gdn_prompt(p_qkvn, p_gb, p_za, ng_row, kt, vt, qkvb,
                                                zb.reshape(B, T, SW_WIDTH), C=DN_CHUNK, TL=_tile(L, GDN_TILE))
    steps = {win // d for win, d in DILATIONS}
    assert len(steps) == 1, "every dilation looks back the same number of class rows"
    blk = steps.pop()
    rt = _tile(L, ATTN_ROWS)
    qbs = dict(zip(dils, (qb1, qb4, qb16)))
    inner = tuple(d for d in dils if rt // d >= blk)
    ext = []
    for d in dils:
        if d not in inner:
            ext += _swa_wide(qbs[d], d=d, blk=blk)
    p_ob = _attn_prompt(tuple(qbs[d] for d in inner), ext, p_zb, dils=inner, blk=blk, rt=rt)
    yp = _out(xp, p_oa, p_ob, w_out16, fg_row, tm=_tile(L, ATTN_ROWS), name="out_prompt")
    prompt = (yp, raw_last[:, HALO - hist:, :], p_s,
              k_last.reshape(Bp, keep, SW_HEADS, SW_HD), v_last.reshape(Bp, keep, SW_HEADS, SW_HD))

    oa, s_new = _gdn_sample(qkvn, gb, front(za), s0, ng_row, C=group, n=_tile(B * group, GDN_TILE))
    oa = oa.reshape(B, group, DN_WIDTH)[:, group - T:].reshape(1, B * T, DN_WIDTH)
    ys = _out(rows, oa, ob.reshape(1, B * T, SW_WIDTH).astype(BF16), w_out16, fg_row,
              tm=_tile(B * T, SAMPLE_ROWS), name="out_sample")
    sample = (ys.reshape(B, T, D_MODEL), jnp.concatenate([conv_state, raw], axis=1)[:, -hist:], s_new,
              jnp.transpose(kt_new.reshape(B, SW_HEADS, SW_HD, W), (0, 3, 1, 2)),
              jnp.transpose(vt_new.reshape(B, SW_HEADS, SW_HD, W), (0, 3, 1, 2)))
    return prompt, sample


def kernel(x_prompt, x_sample, state_conv, state_delta, cache_k_win, cache_v_win, norm_g, final_norm_g,
           w_in, conv_w, a_log, dt_bias, dn_norm_g, w_out):
    assert norm_g.shape[0] == 1, "single layer"
    w = w_in[0]
    o1 = CONV_CH + DN_WIDTH
    o3 = o1 + 2 * DN_HEADS
    o4 = o3 + 3 * SW_WIDTH
    wp = jnp.concatenate(
        [w[:, :o1], w[:, o3:], w[:, o1:o3], jnp.zeros((D_MODEL, LANES - 2 * DN_HEADS), w.dtype)],
        axis=1).astype(BF16)
    assert wp.shape[1] == W_COLS and o4 - o3 == 3 * SW_WIDTH
    g_row = norm_g[0].reshape(1, D_MODEL)
    fg_row = final_norm_g.reshape(1, D_MODEL)
    alog_row = _pad_lanes(a_log[0])
    dtb_row = _pad_lanes(dt_bias[0])
    ng_row = dn_norm_g[0].reshape(1, DN_DV)
    w_out16 = w_out[0].astype(BF16)
    cw = conv_w[0]
    common = (wp, g_row, cw, alog_row, dtb_row, ng_row, w_out16, fg_row)
    (yp, pc, ps, pk, pv), (ys, sc, ss, sk, sv) = _layers(
        x_prompt, x_sample, state_conv[0], state_delta[0], cache_k_win[0], cache_v_win[0], *common)
    return (yp, ys, pc[None], ps[None], pk[None], pv[None], sc[None], ss[None], sk[None], sv[None])
```
